```python
import jax, jax.numpy as jnp
from jax import lax
import numpy as np

D_MODEL = 1024
BATCH = 2
SEQ = 16384
DEPTH = 2

SB_HEADS = 4
SB_HEAD_DIM = 64
Q_BLOCK = 128
RET_HEADS = 4
RET_QK_DIM = 64
RET_V_DIM = 128
RET_CHUNK = 128
ROPE_BASE = 10000.0
POOL_GROUPS = 4
POOL_GROUP_DIM = 64
POOL_WINDOWS = (2, 4, 8, 16)
N_BRANCHES = 3
D_FF = 2816
EPS = 1e-6

SB_W = SB_HEADS * SB_HEAD_DIM
RET_QK_W = RET_HEADS * RET_QK_DIM
RET_V_W = RET_HEADS * RET_V_DIM
POOL_W = POOL_GROUPS * POOL_GROUP_DIM
SPLITS = (SB_W, SB_W, SB_W, RET_QK_W, RET_QK_W, RET_V_W, RET_V_W, POOL_W, N_BRANCHES * D_MODEL)
D_IN = SB_W * 3 + RET_QK_W * 2 + RET_V_W * 2 + POOL_W + N_BRANCHES * D_MODEL

kernel_name = "hybrid_sb_retention_pool_macaron"


def rmsnorm(x, g):
    xf = x.astype(jnp.float32)
    y = xf * lax.rsqrt(jnp.mean(xf * xf, axis=-1, keepdims=True) + EPS)
    return (y * g.astype(jnp.float32)).astype(x.dtype)


def swiglu_half(h, w1, w3, w2):
    return 0.5 * ((jax.nn.silu(h @ w1) * (h @ w3)) @ w2)


def stick_breaking_attention(q, k, v):
    B, S, H, dh = q.shape
    qf = (q.astype(jnp.float32) * (dh ** -0.5)).transpose(0, 2, 1, 3)
    kf = k.astype(jnp.float32).transpose(0, 2, 1, 3)
    vf = v.astype(jnp.float32).transpose(0, 2, 1, 3)
    nb = S // Q_BLOCK
    q_blocks = qf.reshape(B, H, nb, Q_BLOCK, dh).transpose(2, 0, 1, 3, 4)
    key_pos = jnp.arange(S)

    def block(args):
        qb, start = args
        z = jnp.einsum('bhqd,bhkd->bhqk', qb, kf)
        q_pos = start + jnp.arange(Q_BLOCK)
        mask = key_pos[None, :] < q_pos[:, None]
        log_beta = jax.nn.log_sigmoid(z)
        log_fail = jnp.where(mask, log_beta - z, 0.0)
        after = lax.cumsum(log_fail, axis=3, reverse=True) - log_fail
        w = jnp.where(mask, jnp.exp(log_beta + after), 0.0)
        return jnp.einsum('bhqk,bhkd->bhqd', w, vf)

    out = lax.map(block, (q_blocks, jnp.arange(nb) * Q_BLOCK))
    return out.transpose(1, 0, 3, 2, 4).reshape(B, S, H * dh)


def rotary(x, cos, sin):
    half = x.shape[-1] // 2
    x1, x2 = x[..., :half], x[..., half:]
    c = cos[None, :, None, :]
    s = sin[None, :, None, :]
    return jnp.concatenate([x1 * c - x2 * s, x2 * c + x1 * s], axis=-1)


def retention(q, k, v, g):
    B, S, H, dk = q.shape
    dv = v.shape[-1]
    half = dk // 2
    pos = jnp.arange(S, dtype=jnp.float32)
    inv_freq = ROPE_BASE ** (-jnp.arange(half, dtype=jnp.float32) / half)
    ang = pos[:, None] * inv_freq[None, :]
    cos, sin = jnp.cos(ang), jnp.sin(ang)
    qf = rotary(q.astype(jnp.float32), cos, sin)
    kf = rotary(k.astype(jnp.float32), cos, sin) * (dk ** -0.5)
    vf = v.astype(jnp.float32)
    log_gamma = jnp.log(1.0 - 2.0 ** (-5.0 - jnp.arange(H, dtype=jnp.float32)))

    C = RET_CHUNK
    N = S // C
    qc = qf.reshape(B, N, C, H, dk)
    kc = kf.reshape(B, N, C, H, dk)
    vc = vf.reshape(B, N, C, H, dv)
    idx = jnp.arange(C, dtype=jnp.float32)
    diff = idx[:, None] - idx[None, :]
    causal = diff >= 0
    decay = jnp.where(causal[None], jnp.exp(jnp.where(causal, diff, 0.0)[None] * log_gamma[:, None, None]), 0.0)
    scores = jnp.einsum('bnihd,bnjhd->bnhij', qc, kc) * decay
    intra = jnp.einsum('bnhij,bnjhe->bnihe', scores, vc)

    k_decay = jnp.exp((C - 1 - idx)[:, None] * log_gamma[None, :])
    kv = jnp.einsum('bnjhd,bnjhe->nbhde', kc * k_decay[:, :, None], vc)
    chunk_decay = jnp.exp(C * log_gamma)[None, :, None, None]

    def step(state, kv_n):
        return chunk_decay * state + kv_n, state

    _, states = lax.scan(step, jnp.zeros((B, H, dk, dv), jnp.float32), kv)
    q_decay = jnp.exp((idx + 1)[:, None] * log_gamma[None, :])
    cross = jnp.einsum('bnihd,nbhde->bnihe', qc * q_decay[:, :, None], states)

    y = (intra + cross).reshape(B, S, H, dv)
    y = y * lax.rsqrt(jnp.mean(y * y, axis=-1, keepdims=True) + EPS)
    return y.reshape(B, S, H * dv) * jax.nn.silu(g.astype(jnp.float32))


def pool_mixer(u, w_pool, pool_scale):
    B, S, _ = u.shape
    uf = u.astype(jnp.float32).reshape(B, S, POOL_GROUPS, POOL_GROUP_DIM)
    cs = jnp.concatenate([jnp.zeros((B, 1, POOL_GROUPS, POOL_GROUP_DIM), jnp.float32),
                          jnp.cumsum(uf, axis=1)], axis=1)
    t = jnp.arange(S)
    win = jnp.array(POOL_WINDOWS)
    lo = jnp.maximum(t[:, None] + 1 - win[None, :], 0)
    count = (t[:, None] + 1 - lo).astype(jnp.float32)
    cs_lo = cs[:, lo, jnp.arange(POOL_GROUPS)[None, :], :]
    mean = (cs[:, 1:] - cs_lo) / count[None, :, :, None]
    y = jnp.einsum('bsgc,gce->bsge', mean - uf, w_pool.astype(jnp.float32))
    return y.reshape(B, S, POOL_W) * pool_scale.astype(jnp.float32)


def setup_inputs(seed: int = 0) -> dict:
    key = jax.random.key(seed)
    ks = jax.random.split(key, 20)

    def w(k, shape, fan_in):
        return jax.random.normal(k, shape, jnp.float32) * (fan_in ** -0.5)

    def gain(k, shape):
        return 1.0 + 0.02 * jax.random.normal(k, shape, jnp.float32)

    return {
        "x": jax.random.normal(ks[0], (BATCH, SEQ, D_MODEL), jnp.float32),
        "g_ffn1": gain(ks[1], (DEPTH, D_MODEL)),
        "w1_ffn1": w(ks[2], (DEPTH, D_MODEL, D_FF), D_MODEL),
        "w3_ffn1": w(ks[3], (DEPTH, D_MODEL, D_FF), D_MODEL),
        "w2_ffn1": w(ks[4], (DEPTH, D_FF, D_MODEL), D_FF),
        "g_mix": gain(ks[5], (DEPTH, D_MODEL)),
        "w_in": w(ks[6], (DEPTH, D_MODEL, D_IN), D_MODEL),
        "w_branch_sb": w(ks[7], (DEPTH, SB_W, D_MODEL), SB_W),
        "w_branch_ret": w(ks[8], (DEPTH, RET_V_W, D_MODEL), RET_V_W),
        "w_branch_pool": w(ks[9], (DEPTH, POOL_W, D_MODEL), POOL_W),
        "w_pool": w(ks[10], (DEPTH, POOL_GROUPS, POOL_GROUP_DIM, POOL_GROUP_DIM), POOL_GROUP_DIM),
        "pool_scale": gain(ks[11], (DEPTH, POOL_W)),
        "w_out": w(ks[12], (DEPTH, D_MODEL, D_MODEL), D_MODEL),
        "g_ffn2": gain(ks[13], (DEPTH, D_MODEL)),
        "w1_ffn2": w(ks[14], (DEPTH, D_MODEL, D_FF), D_MODEL),
        "w3_ffn2": w(ks[15], (DEPTH, D_MODEL, D_FF), D_MODEL),
        "w2_ffn2": w(ks[16], (DEPTH, D_FF, D_MODEL), D_FF),
        "g_final": gain(ks[17], (D_MODEL,)),
    }


def reference(x, g_ffn1, w1_ffn1, w3_ffn1, w2_ffn1, g_mix, w_in, w_branch_sb, w_branch_ret,
              w_branch_pool, w_pool, pool_scale, w_out, g_ffn2, w1_ffn2, w3_ffn2, w2_ffn2, g_final):
    B, S, D = x.shape
    split_points = [int(p) for p in np.cumsum(SPLITS)[:-1]]
    for l in range(DEPTH):
        x = x + swiglu_half(rmsnorm(x, g_ffn1[l]), w1_ffn1[l], w3_ffn1[l], w2_ffn1[l])

        h = rmsnorm(x, g_mix[l])
        proj = h @ w_in[l]
        q_sb, k_sb, v_sb, q_r, k_r, v_r, g_r, u_p, gate = jnp.split(proj, split_points, axis=-1)
        y_sb = stick_breaking_attention(q_sb.reshape(B, S, SB_HEADS, SB_HEAD_DIM),
                                        k_sb.reshape(B, S, SB_HEADS, SB_HEAD_DIM),
                                        v_sb.reshape(B, S, SB_HEADS, SB_HEAD_DIM))
        y_ret = retention(q_r.reshape(B, S, RET_HEADS, RET_QK_DIM),
                          k_r.reshape(B, S, RET_HEADS, RET_QK_DIM),
                          v_r.reshape(B, S, RET_HEADS, RET_V_DIM), g_r)
        y_pool = pool_mixer(u_p, w_pool[l], pool_scale[l])

        gates = jax.nn.sigmoid(gate.astype(jnp.float32)).reshape(B, S, N_BRANCHES, D)
        merged = (gates[:, :, 0] * (y_sb @ w_branch_sb[l].astype(jnp.float32))
                  + gates[:, :, 1] * (y_ret @ w_branch_ret[l].astype(jnp.float32))
                  + gates[:, :, 2] * (y_pool @ w_branch_pool[l].astype(jnp.float32)))
        x = x + merged.astype(x.dtype) @ w_out[l]

        x = x + swiglu_half(rmsnorm(x, g_ffn2[l]), w1_ffn2[l], w3_ffn2[l], w2_ffn2[l])
    return rmsnorm(x, g_final)
```

```python
import functools
import math

import numpy as np
import jax
import jax.numpy as jnp
from jax import lax
from jax.experimental import pallas as pl
from jax.experimental.pallas import tpu as pltpu

F32 = jnp.float32
BF16 = jnp.bfloat16

D_MODEL = 1024
SB_HEADS = 4
SB_HEAD_DIM = 64
RET_HEADS = 4
RET_QK_DIM = 64
RET_V_DIM = 128
ROPE_BASE = 10000.0
POOL_GROUPS = 4
POOL_GROUP_DIM = 64
POOL_WINDOWS = (2, 4, 8, 16)
N_BRANCHES = 3
D_FF = 2816
EPS = 1e-6

SB_W = SB_HEADS * SB_HEAD_DIM
RET_QK_W = RET_HEADS * RET_QK_DIM
RET_V_W = RET_HEADS * RET_V_DIM
POOL_W = POOL_GROUPS * POOL_GROUP_DIM
D_IN = SB_W * 3 + RET_QK_W * 2 + RET_V_W * 2 + POOL_W + N_BRANCHES * D_MODEL

OFF_QSB = 0
OFF_KSB = OFF_QSB + SB_W
OFF_VSB = OFF_KSB + SB_W
OFF_QR = OFF_VSB + SB_W
OFF_KR = OFF_QR + RET_QK_W
OFF_VR = OFF_KR + RET_QK_W
OFF_GR = OFF_VR + RET_V_W
OFF_UP = OFF_GR + RET_V_W
OFF_GATE = OFF_UP + POOL_W

LOG2E = 1.4426950408889634
LN2 = 0.6931471805599453

SUBLANES = 8
LANES = 128
MXU_DIM = 256
VMEM_LIMIT = 56 * 1024 * 1024

ROW_TILE = 512
FF_CHUNK = 256
SB_Q_TILE = 512
SB_K_TILE = 256
SB_STRANDS = SB_K_TILE // SUBLANES
RET_CHUNK = 256
POOL_HALO = 16
MASK_NEG = -1e30


def _resident(shape):
    nd = len(shape)
    return pl.BlockSpec(shape, lambda *_: (0,) * nd, pipeline_mode=pl.Buffered(1))


def _rms(x, g):
    ms = jnp.mean(x * x, axis=-1, keepdims=True)
    return x * lax.rsqrt(ms + EPS) * g


def _ffn_body(x_ref, g_ref, w1_ref, w3_ref, w2_ref, gf_ref, o_ref, acc_ref, *, final_norm):
    x = x_ref[...]
    h = _rms(x, g_ref[...]).astype(BF16)
    for c in range(D_FF // FF_CHUNK):
        sl = slice(c * FF_CHUNK, (c + 1) * FF_CHUNK)
        a = jnp.dot(h, w1_ref[:, sl], preferred_element_type=F32)
        b = jnp.dot(h, w3_ref[:, sl], preferred_element_type=F32)
        act = (a * jax.nn.sigmoid(a) * b).astype(BF16)
        part = jnp.dot(act, w2_ref[sl, :], preferred_element_type=F32)
        if c == 0:
            acc_ref[...] = part
        else:
            acc_ref[...] += part
    y = x + 0.5 * acc_ref[...]
    if final_norm:
        y = _rms(y, gf_ref[...])
    o_ref[...] = y


def _ffn(x2, g, w1, w3, w2, g_final, final_norm):
    t = x2.shape[0]
    tm = ROW_TILE
    row = pl.BlockSpec((tm, D_MODEL), lambda i: (i, 0))
    return pl.pallas_call(
        functools.partial(_ffn_body, final_norm=final_norm),
        grid=(t // tm,),
        in_specs=[row, _resident((1, D_MODEL)), _resident((D_MODEL, D_FF)), _resident((D_MODEL, D_FF)),
                  _resident((D_FF, D_MODEL)), _resident((1, D_MODEL))],
        out_specs=row,
        out_shape=jax.ShapeDtypeStruct((t, D_MODEL), F32),
        scratch_shapes=[pltpu.VMEM((tm, D_MODEL), F32)],
        compiler_params=pltpu.CompilerParams(dimension_semantics=("parallel",), vmem_limit_bytes=VMEM_LIMIT),
        name="ffn_half",
    )(x2, g, w1, w3, w2, g_final)


def _proj_body(x_ref, g_ref, w_ref, qsb_ref, ksb_ref, vsb_ref, qr_ref, kr_ref, vr_ref, gr_ref, up_ref, gate_ref):
    h = _rms(x_ref[...], g_ref[...]).astype(BF16)

    def mm(lo, width):
        return jnp.dot(h, w_ref[:, lo:lo + width], preferred_element_type=F32)

    qsb_ref[...] = (mm(OFF_QSB, SB_W) * (SB_HEAD_DIM ** -0.5 * LOG2E)).astype(BF16)
    ksb_ref[...] = mm(OFF_KSB, SB_W).astype(BF16)
    vsb_ref[...] = mm(OFF_VSB, SB_W).astype(BF16)
    qr_ref[...] = mm(OFF_QR, RET_QK_W)
    kr_ref[...] = mm(OFF_KR, RET_QK_W)
    vr_ref[...] = mm(OFF_VR, RET_V_W).astype(BF16)
    gr_ref[...] = mm(OFF_GR, RET_V_W)
    up_ref[...] = mm(OFF_UP, POOL_W)
    for c in range(N_BRANCHES):
        gate_ref[:, c * D_MODEL:(c + 1) * D_MODEL] = mm(OFF_GATE + c * D_MODEL, D_MODEL)


def _proj(x2, g, w_in):
    t = x2.shape[0]
    tm = ROW_TILE

    def row(width):
        return pl.BlockSpec((tm, width), lambda i: (i, 0))

    widths = (SB_W, SB_W, SB_W, RET_QK_W, RET_QK_W, RET_V_W, RET_V_W, POOL_W, N_BRANCHES * D_MODEL)
    dtypes = (BF16, BF16, BF16, F32, F32, BF16, F32, F32, F32)
    return pl.pallas_call(
        _proj_body,
        grid=(t // tm,),
        in_specs=[row(D_MODEL), _resident((1, D_MODEL)), _resident((D_MODEL, D_IN))],
        out_specs=[row(w) for w in widths],
        out_shape=[jax.ShapeDtypeStruct((t, w), d) for w, d in zip(widths, dtypes)],
        compiler_params=pltpu.CompilerParams(dimension_semantics=("parallel",), vmem_limit_bytes=VMEM_LIMIT),
        name="mix_proj",
    )(x2, g, w_in)


def _sb_body(q_ref, k_ref, v_ref, d_ref, o_ref, z_scr, c_scr, w_scr, o_scr):
    qi = pl.program_id(2)
    tq = SB_Q_TILE
    q_t = q_ref[0, 0]
    n_tiles = (qi + 1) * (tq // SB_K_TILE)
    o_scr[...] = jnp.zeros_like(o_scr)
    sub = lax.broadcasted_iota(jnp.int32, (SUBLANES, tq), 0)

    def tile(step, carry):
        j = n_tiles - 1 - step
        z = jnp.dot(k_ref[0, 0, j], q_t, preferred_element_type=F32)
        thr = qi * tq - j * SB_K_TILE
        z_scr[...] = jnp.where(d_ref[...] < thr, z, MASK_NEG)

        acc = jnp.zeros((SUBLANES, tq), F32)
        for i in range(SB_STRANDS - 1, -1, -1):
            rows = slice(i * SUBLANES, (i + 1) * SUBLANES)
            zi = z_scr[rows, :]
            sp = jnp.maximum(zi, 0.0) + jnp.log(1.0 + jnp.exp2(-jnp.abs(zi))) * LOG2E
            acc = acc + sp
            c_scr[rows, :] = acc

        y = acc
        for sh in (1, 2, 4):
            y = y + jnp.where(sub + sh < SUBLANES, pltpu.roll(y, SUBLANES - sh, 0), 0.0)
        off = (y - acc) + carry
        off2 = jnp.concatenate([off, off], axis=0)

        for i in range(SB_K_TILE // (2 * SUBLANES)):
            rows = slice(i * 2 * SUBLANES, (i + 1) * 2 * SUBLANES)
            w = jnp.exp2(z_scr[rows, :] - c_scr[rows, :] - off2)
            w_scr[rows, :] = w.astype(BF16)

        o_scr[...] += jnp.dot(v_ref[0, 0, j], w_scr[...], preferred_element_type=F32)
        return carry + jnp.broadcast_to(y[0:1, :], (SUBLANES, tq))

    lax.fori_loop(0, n_tiles, tile, jnp.zeros((SUBLANES, tq), F32))
    o_ref[0, 0] = o_scr[...]


def _sb_attention(q_t, k_p, v_t):
    b, h, dh, s = q_t.shape
    nkt = s // SB_K_TILE
    tq = SB_Q_TILE
    r = np.arange(SB_K_TILE)
    key_in_tile = (r % SUBLANES) * SB_STRANDS + r // SUBLANES
    dmat = jnp.asarray(key_in_tile[:, None] - np.arange(tq)[None, :], dtype=jnp.int32)
    return pl.pallas_call(
        _sb_body,
        grid=(b, h, s // tq),
        in_specs=[
            pl.BlockSpec((1, 1, dh, tq), lambda bi, hi, qi: (bi, hi, 0, qi)),
            pl.BlockSpec((1, 1, nkt, SB_K_TILE, dh), lambda bi, hi, qi: (bi, hi, 0, 0, 0)),
            pl.BlockSpec((1, 1, nkt, dh, SB_K_TILE), lambda bi, hi, qi: (bi, hi, 0, 0, 0)),
            _resident((SB_K_TILE, tq)),
        ],
        out_specs=pl.BlockSpec((1, 1, dh, tq), lambda bi, hi, qi: (bi, hi, 0, qi)),
        out_shape=jax.ShapeDtypeStruct((b, h, dh, s), F32),
        scratch_shapes=[pltpu.VMEM((SB_K_TILE, tq), F32), pltpu.VMEM((SB_K_TILE, tq), F32),
                        pltpu.VMEM((SB_K_TILE, tq), BF16), pltpu.VMEM((dh, tq), F32)],
        compiler_params=pltpu.CompilerParams(dimension_semantics=("parallel", "parallel", "arbitrary"),
                                             vmem_limit_bytes=VMEM_LIMIT),
        name="stick_breaking",
    )(q_t, k_p, v_t, dmat)


def _ret_log_gamma(h):
    return math.log(1.0 - 2.0 ** (-5.0 - h))


def _ret_body(q_ref, k_ref, v_ref, g_ref, cos_ref, sa_ref, sb_ref, dec_ref, qd_ref, kd_ref, o_ref, st_scr):
    c = RET_CHUNK

    @pl.when(pl.program_id(1) == 0)
    def _():
        st_scr[...] = jnp.zeros_like(st_scr)

    cos = cos_ref[...]
    sa = sa_ref[...]
    sb = sb_ref[...]

    def rot(x):
        return x * cos + pltpu.roll(x, RET_QK_DIM // 2, 1) * sa + pltpu.roll(x, RET_QK_W - RET_QK_DIM // 2, 1) * sb

    q = rot(q_ref[...])
    k = rot(k_ref[...])
    kb = k.astype(BF16)
    qdb = (q * qd_ref[...]).astype(BF16)
    kdt = (k * kd_ref[...]).T.astype(BF16)
    v = v_ref[...]
    lane = lax.broadcasted_iota(jnp.int32, (c, RET_QK_W), 1)
    st = st_scr[...]
    stb = st.astype(BF16)
    kv_all = jnp.dot(kdt, v, preferred_element_type=F32)

    for h in range(RET_HEADS):
        head = (lane >= h * RET_QK_DIM) & (lane < (h + 1) * RET_QK_DIM)
        qh = jnp.where(head, q, 0.0).astype(BF16)
        scores = lax.dot_general(qh, kb, (((1,), (1,)), ((), ())), preferred_element_type=F32)
        scores = (scores * dec_ref[h]).astype(BF16)
        vh = v[:, h * RET_V_DIM:(h + 1) * RET_V_DIM]
        intra = jnp.dot(scores, vh, preferred_element_type=F32)
        qdh = jnp.where(head, qdb, jnp.zeros_like(qdb))
        cross = jnp.dot(qdh, stb, preferred_element_type=F32)
        y = intra + cross
        y = y * lax.rsqrt(jnp.mean(y * y, axis=-1, keepdims=True) + EPS)
        gh = g_ref[:, h * RET_V_DIM:(h + 1) * RET_V_DIM]
        o_ref[:, h * RET_V_DIM:(h + 1) * RET_V_DIM] = (y * (gh * jax.nn.sigmoid(gh))).astype(BF16)
        rows = slice(h * RET_QK_DIM, (h + 1) * RET_QK_DIM)
        st_scr[rows, :] = (math.exp(c * _ret_log_gamma(h)) * st[rows, :]
                           + kv_all[rows, h * RET_V_DIM:(h + 1) * RET_V_DIM])


def _ret_tables(s):
    half = RET_QK_DIM // 2
    pos = jnp.arange(s, dtype=F32)
    inv_freq = ROPE_BASE ** (-jnp.arange(half, dtype=F32) / half)
    ang = pos[:, None] * inv_freq[None, :]
    cos, sin = jnp.cos(ang), jnp.sin(ang)
    zero = jnp.zeros_like(sin)
    cos_t = jnp.tile(jnp.concatenate([cos, cos], axis=1), (1, RET_HEADS))
    sa_t = jnp.tile(jnp.concatenate([zero, sin], axis=1), (1, RET_HEADS))
    sb_t = jnp.tile(jnp.concatenate([-sin, zero], axis=1), (1, RET_HEADS))

    c = RET_CHUNK
    idx = np.arange(c, dtype=np.float64)
    lg = np.array([_ret_log_gamma(h) for h in range(RET_HEADS)])
    diff = idx[:, None] - idx[None, :]
    scale = RET_QK_DIM ** -0.5
    dec = np.where(diff >= 0, np.exp(np.maximum(diff, 0.0)[None] * lg[:, None, None]), 0.0) * scale
    qd = np.repeat(np.exp((idx + 1)[:, None] * lg[None, :]), RET_QK_DIM, axis=1)
    kd = np.repeat(np.exp((c - 1 - idx)[:, None] * lg[None, :]), RET_QK_DIM, axis=1) * scale
    return cos_t, sa_t, sb_t, jnp.asarray(dec, F32), jnp.asarray(qd, F32), jnp.asarray(kd, F32)


def _retention(q, k, v, g, tables, b, s):
    c = RET_CHUNK
    nc = s // c
    cos_t, sa_t, sb_t, dec, qd, kd = tables

    def row(width):
        return pl.BlockSpec((c, width), lambda bi, ci: (bi * nc + ci, 0))

    def pos(width):
        return pl.BlockSpec((c, width), lambda bi, ci: (ci, 0))

    return pl.pallas_call(
        _ret_body,
        grid=(b, nc),
        in_specs=[row(RET_QK_W), row(RET_QK_W), row(RET_V_W), row(RET_V_W),
                  pos(RET_QK_W), pos(RET_QK_W), pos(RET_QK_W),
                  _resident((RET_HEADS, c, c)), _resident((c, RET_QK_W)), _resident((c, RET_QK_W))],
        out_specs=row(RET_V_W),
        out_shape=jax.ShapeDtypeStruct((b * s, RET_V_W), BF16),
        scratch_shapes=[pltpu.VMEM((RET_QK_W, RET_V_DIM), F32)],
        compiler_params=pltpu.CompilerParams(dimension_semantics=("parallel", "arbitrary"),
                                             vmem_limit_bytes=VMEM_LIMIT),
        name="retention",
    )(q, k, v, g, cos_t, sa_t, sb_t, dec, qd, kd)


def _merge_body(x_ref, ysb_ref, yret_ref, u_ref, halo_ref, gate_ref, wsb_ref, wret_ref, wpool_ref, pscale_ref,
                wbp_ref, wout_ref, o_ref, *, tiles_per_seq):
    tm = ROW_TILE
    i = pl.program_id(0)
    first = (i % tiles_per_seq) == 0
    u = u_ref[...]
    halo = jnp.where(first, 0.0, halo_ref[...])
    ext = jnp.concatenate([halo, u], axis=0)
    s2 = ext + pltpu.roll(ext, 1, 0)
    s4 = s2 + pltpu.roll(s2, 2, 0)
    s8 = s4 + pltpu.roll(s4, 4, 0)
    s16 = s8 + pltpu.roll(s8, 8, 0)
    lane = lax.broadcasted_iota(jnp.int32, (tm, POOL_W), 1)
    grp = lane // POOL_GROUP_DIM
    h0 = POOL_HALO
    win_sum = jnp.where(grp == 0, s2[h0:], jnp.where(grp == 1, s4[h0:], jnp.where(grp == 2, s8[h0:], s16[h0:])))
    window = jnp.where(grp == 0, POOL_WINDOWS[0],
                       jnp.where(grp == 1, POOL_WINDOWS[1], jnp.where(grp == 2, POOL_WINDOWS[2], POOL_WINDOWS[3])))
    t_pos = (i % tiles_per_seq) * tm + lax.broadcasted_iota(jnp.int32, (tm, POOL_W), 0)
    count = jnp.minimum(t_pos + 1, window).astype(F32)
    pooled = (win_sum / count - u).astype(BF16)
    y_pool = jnp.dot(pooled, wpool_ref[...], preferred_element_type=F32) * pscale_ref[...]

    def gate(c):
        return jax.nn.sigmoid(gate_ref[:, c * D_MODEL:(c + 1) * D_MODEL])

    merged = gate(0) * jnp.dot(ysb_ref[...], wsb_ref[...], preferred_element_type=F32)
    merged += gate(1) * jnp.dot(yret_ref[...], wret_ref[...], preferred_element_type=F32)
    merged += gate(2) * jnp.dot(y_pool.astype(BF16), wbp_ref[...], preferred_element_type=F32)
    o_ref[...] = x_ref[...] + jnp.dot(merged.astype(BF16), wout_ref[...], preferred_element_type=F32)


def _merge(x2, y_sb, y_ret, u, gate, wsb, wret, wpool_bd, pscale, wbp, wout, s):
    t = x2.shape[0]
    tm = ROW_TILE
    halo_per_tile = tm // POOL_HALO

    def row(width):
        return pl.BlockSpec((tm, width), lambda i: (i, 0))

    halo = pl.BlockSpec((POOL_HALO, POOL_W), lambda i: (jnp.maximum(i * halo_per_tile - 1, 0), 0))
    return pl.pallas_call(
        functools.partial(_merge_body, tiles_per_seq=s // tm),
        grid=(t // tm,),
        in_specs=[row(D_MODEL), row(SB_W), row(RET_V_W), row(POOL_W), halo, row(N_BRANCHES * D_MODEL),
                  _resident((SB_W, D_MODEL)), _resident((RET_V_W, D_MODEL)), _resident((POOL_W, POOL_W)),
                  _resident((1, POOL_W)), _resident((POOL_W, D_MODEL)), _resident((D_MODEL, D_MODEL))],
        out_specs=row(D_MODEL),
        out_shape=jax.ShapeDtypeStruct((t, D_MODEL), F32),
        compiler_params=pltpu.CompilerParams(dimension_semantics=("parallel",), vmem_limit_bytes=VMEM_LIMIT),
        name="merge_out",
    )(x2, y_sb, y_ret, u, u, gate, wsb, wret, wpool_bd, pscale, wbp, wout)


def _block_diag(w_pool):
    out = jnp.zeros((POOL_W, POOL_W), w_pool.dtype)
    for g in range(POOL_GROUPS):
        sl = slice(g * POOL_GROUP_DIM, (g + 1) * POOL_GROUP_DIM)
        out = out.at[sl, sl].set(w_pool[g])
    return out


def kernel(x, g_ffn1, w1_ffn1, w3_ffn1, w2_ffn1, g_mix, w_in, w_branch_sb, w_branch_ret, w_branch_pool, w_pool,
           pool_scale, w_out, g_ffn2, w1_ffn2, w3_ffn2, w2_ffn2, g_final):
    b, s, d = x.shape
    assert d == D_MODEL and s % SB_Q_TILE == 0 and s % RET_CHUNK == 0 and s % ROW_TILE == 0
    depth = g_ffn1.shape[0]
    t = b * s
    nkt = s // SB_K_TILE
    x2 = x.reshape(t, d)
    tables = _ret_tables(s)
    gf = g_final.reshape(1, d)

    def bf(w):
        return w.astype(BF16)

    for l in range(depth):
        x2 = _ffn(x2, g_ffn1[l].reshape(1, d), bf(w1_ffn1[l]), bf(w3_ffn1[l]), bf(w2_ffn1[l]), gf, False)

        q_sb, k_sb, v_sb, q_r, k_r, v_r, g_r, u_p, gate = _proj(x2, g_mix[l].reshape(1, d), bf(w_in[l]))

        q_t = q_sb.reshape(b, s, SB_HEADS, SB_HEAD_DIM).transpose(0, 2, 3, 1)
        k_p = (k_sb.reshape(b, nkt, SUBLANES, SB_STRANDS, SB_HEADS, SB_HEAD_DIM)
               .transpose(0, 4, 1, 3, 2, 5).reshape(b, SB_HEADS, nkt, SB_K_TILE, SB_HEAD_DIM))
        v_t = (v_sb.reshape(b, nkt, SUBLANES, SB_STRANDS, SB_HEADS, SB_HEAD_DIM)
               .transpose(0, 4, 1, 5, 3, 2).reshape(b, SB_HEADS, nkt, SB_HEAD_DIM, SB_K_TILE))
        o_t = _sb_attention(q_t, k_p, v_t)
        y_sb = o_t.transpose(0, 3, 1, 2).reshape(t, SB_W).astype(BF16)

        y_ret = _retention(q_r, k_r, v_r, g_r, tables, b, s)

        x2 = _merge(x2, y_sb, y_ret, u_p, gate, bf(w_branch_sb[l]), bf(w_branch_ret[l]), bf(_block_diag(w_pool[l])),
                    pool_scale[l].reshape(1, POOL_W), bf(w_branch_pool[l]), bf(w_out[l]), s)

        x2 = _ffn(x2, g_ffn2[l].reshape(1, d), bf(w1_ffn2[l]), bf(w3_ffn2[l]), bf(w2_ffn2[l]), gf,
                  l == depth - 1)
    return x2.reshape(b, s, d)
```

```python
import functools
import math

import numpy as np
import jax
import jax.numpy as jnp
from jax import lax
from jax.experimental import pallas as pl
from jax.experimental.pallas import tpu as pltpu

F32 = jnp.float32
BF16 = jnp.bfloat16

D_MODEL = 1024
SB_HEADS = 4
SB_HEAD_DIM = 64
RET_HEADS = 4
RET_QK_DIM = 64
RET_V_DIM = 128
ROPE_BASE = 10000.0
POOL_GROUPS = 4
POOL_GROUP_DIM = 64
POOL_WINDOWS = (2, 4, 8, 16)
N_BRANCHES = 3
D_FF = 2816
EPS = 1e-6

SB_W = SB_HEADS * SB_HEAD_DIM
RET_QK_W = RET_HEADS * RET_QK_DIM
RET_V_W = RET_HEADS * RET_V_DIM
POOL_W = POOL_GROUPS * POOL_GROUP_DIM
D_IN = SB_W * 3 + RET_QK_W * 2 + RET_V_W * 2 + POOL_W + N_BRANCHES * D_MODEL

OFF_QSB = 0
OFF_KSB = OFF_QSB + SB_W
OFF_VSB = OFF_KSB + SB_W
OFF_QR = OFF_VSB + SB_W
OFF_KR = OFF_QR + RET_QK_W
OFF_VR = OFF_KR + RET_QK_W
OFF_GR = OFF_VR + RET_V_W
OFF_UP = OFF_GR + RET_V_W
OFF_GATE = OFF_UP + POOL_W

SUBLANES = 8
LANES = 128
MXU_DIM = 256
VMEM_LIMIT = 56 * 1024 * 1024

ROW_TILE = 512
FF_CHUNK = 256
SB_Q_TILE = 512
SB_K_TILE = 256
SB_STRANDS = SB_K_TILE // SUBLANES
RET_CHUNK = 256
POOL_HALO = 16


def _resident(shape):
    nd = len(shape)
    return pl.BlockSpec(shape, lambda *_: (0,) * nd, pipeline_mode=pl.Buffered(1))


def _rms(x, g):
    ms = jnp.mean(x * x, axis=-1, keepdims=True)
    return x * lax.rsqrt(ms + EPS) * g


def _ffn_body(x_ref, g_ref, w1_ref, w3_ref, w2_ref, gf_ref, o_ref, acc_ref, *, final_norm):
    x = x_ref[...]
    h = _rms(x, g_ref[...]).astype(BF16)
    for c in range(D_FF // FF_CHUNK):
        sl = slice(c * FF_CHUNK, (c + 1) * FF_CHUNK)
        a = jnp.dot(h, w1_ref[:, sl], preferred_element_type=F32)
        b = jnp.dot(h, w3_ref[:, sl], preferred_element_type=F32)
        act = (a * jax.nn.sigmoid(a) * b).astype(BF16)
        part = jnp.dot(act, w2_ref[sl, :], preferred_element_type=F32)
        if c == 0:
            acc_ref[...] = part
        else:
            acc_ref[...] += part
    y = x + 0.5 * acc_ref[...]
    if final_norm:
        y = _rms(y, gf_ref[...])
    o_ref[...] = y


def _ffn(x2, g, w1, w3, w2, g_final, final_norm):
    t = x2.shape[0]
    tm = ROW_TILE
    row = pl.BlockSpec((tm, D_MODEL), lambda i: (i, 0))
    return pl.pallas_call(
        functools.partial(_ffn_body, final_norm=final_norm),
        grid=(t // tm,),
        in_specs=[row, _resident((1, D_MODEL)), _resident((D_MODEL, D_FF)), _resident((D_MODEL, D_FF)),
                  _resident((D_FF, D_MODEL)), _resident((1, D_MODEL))],
        out_specs=row,
        out_shape=jax.ShapeDtypeStruct((t, D_MODEL), F32),
        scratch_shapes=[pltpu.VMEM((tm, D_MODEL), F32)],
        compiler_params=pltpu.CompilerParams(dimension_semantics=("parallel",), vmem_limit_bytes=VMEM_LIMIT),
        name="ffn_half",
    )(x2, g, w1, w3, w2, g_final)


def _proj_body(x_ref, g_ref, w_ref, qsb_ref, ksb_ref, vsb_ref, qr_ref, kr_ref, vr_ref, gr_ref, up_ref, gate_ref,
               perm_scr):
    h = _rms(x_ref[...], g_ref[...]).astype(BF16)

    def mm(lo, width):
        return jnp.dot(h, w_ref[:, lo:lo + width], preferred_element_type=F32)

    def stage(vals):
        for c in range(SB_W // LANES):
            perm_scr[c] = vals[:, c * LANES:(c + 1) * LANES]

    def strand_major(kt):
        base = kt * SB_K_TILE
        return jnp.concatenate(
            [jnp.concatenate([perm_scr[c, pl.ds(base + i, SUBLANES, stride=SB_STRANDS), :]
                              for i in range(SB_STRANDS)], axis=0) for c in range(SB_W // LANES)], axis=1)

    qsb_ref[0] = (mm(OFF_QSB, SB_W) * (SB_HEAD_DIM ** -0.5 * 0.5)).T.astype(BF16)
    stage(mm(OFF_KSB, SB_W))
    for kt in range(ROW_TILE // SB_K_TILE):
        ksb_ref[0, kt] = strand_major(kt).astype(BF16)
    stage(mm(OFF_VSB, SB_W))
    for kt in range(ROW_TILE // SB_K_TILE):
        vsb_ref[0, kt] = strand_major(kt).T.astype(BF16)
    qr_ref[...] = mm(OFF_QR, RET_QK_W)
    kr_ref[...] = mm(OFF_KR, RET_QK_W)
    vr_ref[...] = mm(OFF_VR, RET_V_W).astype(BF16)
    gr_ref[...] = mm(OFF_GR, RET_V_W)
    up_ref[...] = mm(OFF_UP, POOL_W)
    for c in range(N_BRANCHES):
        gate_ref[:, c * D_MODEL:(c + 1) * D_MODEL] = mm(OFF_GATE + c * D_MODEL, D_MODEL)


def _proj(x2, g, w_in, b, s):
    t = x2.shape[0]
    tm = ROW_TILE
    tps = s // tm
    kt_per_tile = tm // SB_K_TILE
    nkt = s // SB_K_TILE

    def row(width):
        return pl.BlockSpec((tm, width), lambda i: (i, 0))

    key_tiles = pl.BlockSpec((1, kt_per_tile, SB_K_TILE, SB_W), lambda i: (i // tps, i % tps, 0, 0))
    widths = (RET_QK_W, RET_QK_W, RET_V_W, RET_V_W, POOL_W, N_BRANCHES * D_MODEL)
    dtypes = (F32, F32, BF16, F32, F32, F32)
    return pl.pallas_call(
        _proj_body,
        grid=(t // tm,),
        in_specs=[row(D_MODEL), _resident((1, D_MODEL)), _resident((D_MODEL, D_IN))],
        out_specs=[pl.BlockSpec((1, SB_W, tm), lambda i: (i // tps, 0, i % tps)), key_tiles, key_tiles]
                  + [row(w) for w in widths],
        out_shape=[jax.ShapeDtypeStruct((b, SB_W, s), BF16),
                   jax.ShapeDtypeStruct((b, nkt, SB_K_TILE, SB_W), BF16),
                   jax.ShapeDtypeStruct((b, nkt, SB_W, SB_K_TILE), BF16)]
                  + [jax.ShapeDtypeStruct((t, w), d) for w, d in zip(widths, dtypes)],
        scratch_shapes=[pltpu.VMEM((SB_W // LANES, tm, LANES), F32)],
        compiler_params=pltpu.CompilerParams(dimension_semantics=("parallel",), vmem_limit_bytes=VMEM_LIMIT),
        name="mix_proj",
    )(x2, g, w_in)


def _sb_body(q_ref, k_ref, v_ref, d_ref, o_ref, q_scr, z_scr, p_scr, w_scr, o_scr):
    hi = pl.program_id(1)
    qi = pl.program_id(2)
    tq = SB_Q_TILE
    n_tiles = (qi + 1) * (tq // SB_K_TILE)
    o_scr[...] = jnp.zeros_like(o_scr)
    q_scr[...] = jnp.zeros_like(q_scr)
    q_scr[pl.ds(pl.multiple_of(hi * SB_HEAD_DIM, SB_HEAD_DIM), SB_HEAD_DIM), :] = q_ref[0]
    sub = lax.broadcasted_iota(jnp.int32, (SUBLANES, tq), 0)

    def logits(j, slot):
        z_scr[slot] = jnp.dot(k_ref[0, j], q_scr[...], preferred_element_type=F32)

    def weights(j, slot, carry, masked):
        thr = qi * tq - j * SB_K_TILE
        acc = jnp.ones((SUBLANES, tq), F32)
        for i in range(SB_STRANDS - 1, -1, -1):
            rows = slice(i * SUBLANES, (i + 1) * SUBLANES)
            th = jnp.tanh(z_scr[slot, rows, :])
            if masked:
                th = jnp.where(d_ref[rows, :] < thr, th, -1.0)
            half = 0.5 * th
            p_scr[rows, :] = (0.5 + half) * acc
            acc = acc * (0.5 - half)
        y = acc
        for sh in (1, 2, 4):
            y = y * jnp.where(sub + sh < SUBLANES, pltpu.roll(y, SUBLANES - sh, 0), 1.0)
        off = jnp.where(sub + 1 < SUBLANES, pltpu.roll(y, SUBLANES - 1, 0), 1.0) * carry
        off2 = jnp.concatenate([off, off], axis=0)
        for i in range(SB_K_TILE // (2 * SUBLANES)):
            rows = slice(i * 2 * SUBLANES, (i + 1) * 2 * SUBLANES)
            w_scr[slot, rows, :] = (p_scr[rows, :] * off2).astype(BF16)
        return carry * jnp.broadcast_to(y[0:1, :], (SUBLANES, tq))

    def values(j, slot):
        o_scr[...] += jnp.dot(v_ref[0, j], w_scr[slot], preferred_element_type=F32)

    top = n_tiles - 1
    carry = jnp.ones((SUBLANES, tq), F32)
    logits(top, 0)
    logits(top - 1, 1)
    carry = weights(top, 0, carry, True)
    logits(jnp.maximum(top - 2, 0), 0)
    carry = weights(top - 1, 1, carry, True)
    values(top, 0)

    def pair(p, carry):
        j = top - 2 * p
        logits(j - 1, 1)
        carry = weights(j, 0, carry, False)
        values(j + 1, 1)
        logits(jnp.maximum(j - 2, 0), 0)
        carry = weights(j - 1, 1, carry, False)
        values(j, 0)
        return carry

    lax.fori_loop(1, n_tiles // 2, pair, carry)
    values(0, 1)
    o_ref[0] = o_scr[...]


def _sb_attention(q_t, k_p, v_t):
    b, _, s = q_t.shape
    dh = SB_HEAD_DIM
    nkt = s // SB_K_TILE
    tq = SB_Q_TILE
    r = np.arange(SB_K_TILE)
    key_in_tile = (r % SUBLANES) * SB_STRANDS + r // SUBLANES
    dmat = jnp.asarray(key_in_tile[:, None] - np.arange(tq)[None, :], dtype=jnp.int32)
    return pl.pallas_call(
        _sb_body,
        grid=(b, SB_HEADS, s // tq),
        in_specs=[
            pl.BlockSpec((1, dh, tq), lambda bi, hi, qi: (bi, hi, qi)),
            pl.BlockSpec((1, nkt, SB_K_TILE, SB_W), lambda bi, hi, qi: (bi, 0, 0, 0)),
            pl.BlockSpec((1, nkt, dh, SB_K_TILE), lambda bi, hi, qi: (bi, 0, hi, 0)),
            _resident((SB_K_TILE, tq)),
        ],
        out_specs=pl.BlockSpec((1, dh, tq), lambda bi, hi, qi: (bi, hi, qi)),
        out_shape=jax.ShapeDtypeStruct((b, SB_W, s), F32),
        scratch_shapes=[pltpu.VMEM((SB_W, tq), BF16), pltpu.VMEM((2, SB_K_TILE, tq), F32),
                        pltpu.VMEM((SB_K_TILE, tq), F32), pltpu.VMEM((2, SB_K_TILE, tq), BF16),
                        pltpu.VMEM((dh, tq), F32)],
        compiler_params=pltpu.CompilerParams(dimension_semantics=("parallel", "parallel", "arbitrary"),
                                             vmem_limit_bytes=VMEM_LIMIT),
        name="stick_breaking",
    )(q_t, k_p, v_t, dmat)


def _ret_log_gamma(h):
    return math.log(1.0 - 2.0 ** (-5.0 - h))


def _ret_body(q_ref, k_ref, v_ref, g_ref, cos_ref, sa_ref, sb_ref, dec_ref, qd_ref, kd_ref, o_ref, st_scr):
    c = RET_CHUNK

    @pl.when(pl.program_id(1) == 0)
    def _():
        st_scr[...] = jnp.zeros_like(st_scr)

    cos = cos_ref[...]
    sa = sa_ref[...]
    sb = sb_ref[...]

    def rot(x):
        return x * cos + pltpu.roll(x, RET_QK_DIM // 2, 1) * sa + pltpu.roll(x, RET_QK_W - RET_QK_DIM // 2, 1) * sb

    q = rot(q_ref[...])
    k = rot(k_ref[...])
    kb = k.astype(BF16)
    qdb = (q * qd_ref[...]).astype(BF16)
    kdt = (k * kd_ref[...]).T.astype(BF16)
    v = v_ref[...]
    lane = lax.broadcasted_iota(jnp.int32, (c, RET_QK_W), 1)
    st = st_scr[...]
    stb = st.astype(BF16)
    kv_all = jnp.dot(kdt, v, preferred_element_type=F32)

    for h in range(RET_HEADS):
        head = (lane >= h * RET_QK_DIM) & (lane < (h + 1) * RET_QK_DIM)
        qh = jnp.where(head, q, 0.0).astype(BF16)
        scores = lax.dot_general(qh, kb, (((1,), (1,)), ((), ())), preferred_element_type=F32)
        scores = (scores * dec_ref[h]).astype(BF16)
        vh = v[:, h * RET_V_DIM:(h + 1) * RET_V_DIM]
        intra = jnp.dot(scores, vh, preferred_element_type=F32)
        qdh = jnp.where(head, qdb, jnp.zeros_like(qdb))
        cross = jnp.dot(qdh, stb, preferred_element_type=F32)
        y = intra + cross
        y = y * lax.rsqrt(jnp.mean(y * y, axis=-1, keepdims=True) + EPS)
        gh = g_ref[:, h * RET_V_DIM:(h + 1) * RET_V_DIM]
        o_ref[:, h * RET_V_DIM:(h + 1) * RET_V_DIM] = (y * (gh * jax.nn.sigmoid(gh))).astype(BF16)
        rows = slice(h * RET_QK_DIM, (h + 1) * RET_QK_DIM)
        st_scr[rows, :] = (math.exp(c * _ret_log_gamma(h)) * st[rows, :]
                           + kv_all[rows, h * RET_V_DIM:(h + 1) * RET_V_DIM])


def _ret_tables(s):
    half = RET_QK_DIM // 2
    pos = jnp.arange(s, dtype=F32)
    inv_freq = ROPE_BASE ** (-jnp.arange(half, dtype=F32) / half)
    ang = pos[:, None] * inv_freq[None, :]
    cos, sin = jnp.cos(ang), jnp.sin(ang)
    zero = jnp.zeros_like(sin)
    cos_t = jnp.tile(jnp.concatenate([cos, cos], axis=1), (1, RET_HEADS))
    sa_t = jnp.tile(jnp.concatenate([zero, sin], axis=1), (1, RET_HEADS))
    sb_t = jnp.tile(jnp.concatenate([-sin, zero], axis=1), (1, RET_HEADS))

    c = RET_CHUNK
    idx = np.arange(c, dtype=np.float64)
    lg = np.array([_ret_log_gamma(h) for h in range(RET_HEADS)])
    diff = idx[:, None] - idx[None, :]
    scale = RET_QK_DIM ** -0.5
    dec = np.where(diff >= 0, np.exp(np.maximum(diff, 0.0)[None] * lg[:, None, None]), 0.0) * scale
    qd = np.repeat(np.exp((idx + 1)[:, None] * lg[None, :]), RET_QK_DIM, axis=1)
    kd = np.repeat(np.exp((c - 1 - idx)[:, None] * lg[None, :]), RET_QK_DIM, axis=1) * scale
    return cos_t, sa_t, sb_t, jnp.asarray(dec, F32), jnp.asarray(qd, F32), jnp.asarray(kd, F32)


def _retention(q, k, v, g, tables, b, s):
    c = RET_CHUNK
    nc = s // c
    cos_t, sa_t, sb_t, dec, qd, kd = tables

    def row(width):
        return pl.BlockSpec((c, width), lambda bi, ci: (bi * nc + ci, 0))

    def pos(width):
        return pl.BlockSpec((c, width), lambda bi, ci: (ci, 0))

    return pl.pallas_call(
        _ret_body,
        grid=(b, nc),
        in_specs=[row(RET_QK_W), row(RET_QK_W), row(RET_V_W), row(RET_V_W),
                  pos(RET_QK_W), pos(RET_QK_W), pos(RET_QK_W),
                  _resident((RET_HEADS, c, c)), _resident((c, RET_QK_W)), _resident((c, RET_QK_W))],
        out_specs=row(RET_V_W),
        out_shape=jax.ShapeDtypeStruct((b * s, RET_V_W), BF16),
        scratch_shapes=[pltpu.VMEM((RET_QK_W, RET_V_DIM), F32)],
        compiler_params=pltpu.CompilerParams(dimension_semantics=("parallel", "arbitrary"),
                                             vmem_limit_bytes=VMEM_LIMIT),
        name="retention",
    )(q, k, v, g, cos_t, sa_t, sb_t, dec, qd, kd)


def _merge_body(x_ref, ysb_ref, yret_ref, u_ref, halo_ref, gate_ref, wsb_ref, wret_ref, wpool_ref, pscale_ref,
                wbp_ref, wout_ref, o_ref, *, tiles_per_seq):
    tm = ROW_TILE
    i = pl.program_id(0)
    first = (i % tiles_per_seq) == 0
    u = u_ref[...]
    halo = jnp.where(first, 0.0, halo_ref[...])
    ext = jnp.concatenate([halo, u], axis=0)
    s2 = ext + pltpu.roll(ext, 1, 0)
    s4 = s2 + pltpu.roll(s2, 2, 0)
    s8 = s4 + pltpu.roll(s4, 4, 0)
    s16 = s8 + pltpu.roll(s8, 8, 0)
    lane = lax.broadcasted_iota(jnp.int32, (tm, POOL_W), 1)
    grp = lane // POOL_GROUP_DIM
    h0 = POOL_HALO
    win_sum = jnp.where(grp == 0, s2[h0:], jnp.where(grp == 1, s4[h0:], jnp.where(grp == 2, s8[h0:], s16[h0:])))
    window = jnp.where(grp == 0, POOL_WINDOWS[0],
                       jnp.where(grp == 1, POOL_WINDOWS[1], jnp.where(grp == 2, POOL_WINDOWS[2], POOL_WINDOWS[3])))
    t_pos = (i % tiles_per_seq) * tm + lax.broadcasted_iota(jnp.int32, (tm, POOL_W), 0)
    count = jnp.minimum(t_pos + 1, window).astype(F32)
    pooled = (win_sum / count - u).astype(BF16)
    y_pool = jnp.dot(pooled, wpool_ref[...], preferred_element_type=F32) * pscale_ref[...]

    def gate(c):
        return jax.nn.sigmoid(gate_ref[:, c * D_MODEL:(c + 1) * D_MODEL])

    y_sb = ysb_ref[0].T.astype(BF16)
    merged = gate(0) * jnp.dot(y_sb, wsb_ref[...], preferred_element_type=F32)
    merged += gate(1) * jnp.dot(yret_ref[...], wret_ref[...], preferred_element_type=F32)
    merged += gate(2) * jnp.dot(y_pool.astype(BF16), wbp_ref[...], preferred_element_type=F32)
    o_ref[...] = x_ref[...] + jnp.dot(merged.astype(BF16), wout_ref[...], preferred_element_type=F32)


def _merge(x2, y_sb, y_ret, u, gate, wsb, wret, wpool_bd, pscale, wbp, wout, s):
    t = x2.shape[0]
    tm = ROW_TILE
    halo_per_tile = tm // POOL_HALO
    tps = s // tm

    def row(width):
        return pl.BlockSpec((tm, width), lambda i: (i, 0))

    halo = pl.BlockSpec((POOL_HALO, POOL_W), lambda i: (jnp.maximum(i * halo_per_tile - 1, 0), 0))
    return pl.pallas_call(
        functools.partial(_merge_body, tiles_per_seq=s // tm),
        grid=(t // tm,),
        in_specs=[row(D_MODEL), pl.BlockSpec((1, SB_W, tm), lambda i: (i // tps, 0, i % tps)),
                  row(RET_V_W), row(POOL_W), halo, row(N_BRANCHES * D_MODEL),
                  _resident((SB_W, D_MODEL)), _resident((RET_V_W, D_MODEL)), _resident((POOL_W, POOL_W)),
                  _resident((1, POOL_W)), _resident((POOL_W, D_MODEL)), _resident((D_MODEL, D_MODEL))],
        out_specs=row(D_MODEL),
        out_shape=jax.ShapeDtypeStruct((t, D_MODEL), F32),
        compiler_params=pltpu.CompilerParams(dimension_semantics=("parallel",), vmem_limit_bytes=VMEM_LIMIT),
        name="merge_out",
    )(x2, y_sb, y_ret, u, u, gate, wsb, wret, wpool_bd, pscale, wbp, wout)


def _block_diag(w_pool):
    out = jnp.zeros((POOL_W, POOL_W), w_pool.dtype)
    for g in range(POOL_GROUPS):
        sl = slice(g * POOL_GROUP_DIM, (g + 1) * POOL_GROUP_DIM)
        out = out.at[sl, sl].set(w_pool[g])
    return out


def kernel(x, g_ffn1, w1_ffn1, w3_ffn1, w2_ffn1, g_mix, w_in, w_branch_sb, w_branch_ret, w_branch_pool, w_pool,
           pool_scale, w_out, g_ffn2, w1_ffn2, w3_ffn2, w2_ffn2, g_final):
    b, s, d = x.shape
    assert d == D_MODEL and s % SB_Q_TILE == 0 and s % RET_CHUNK == 0 and s % ROW_TILE == 0
    depth = g_ffn1.shape[0]
    t = b * s
    x2 = x.reshape(t, d)
    tables = _ret_tables(s)
    gf = g_final.reshape(1, d)

    def bf(w):
        return w.astype(BF16)

    for l in range(depth):
        x2 = _ffn(x2, g_ffn1[l].reshape(1, d), bf(w1_ffn1[l]), bf(w3_ffn1[l]), bf(w2_ffn1[l]), gf, False)

        q_t, k_p, v_t, q_r, k_r, v_r, g_r, u_p, gate = _proj(x2, g_mix[l].reshape(1, d), bf(w_in[l]), b, s)
        y_sb = _sb_attention(q_t, k_p, v_t)

        y_ret = _retention(q_r, k_r, v_r, g_r, tables, b, s)

        x2 = _merge(x2, y_sb, y_ret, u_p, gate, bf(w_branch_sb[l]), bf(w_branch_ret[l]), bf(_block_diag(w_pool[l])),
                    pool_scale[l].reshape(1, POOL_W), bf(w_branch_pool[l]), bf(w_out[l]), s)

        x2 = _ffn(x2, g_ffn2[l].reshape(1, d), bf(w1_ffn2[l]), bf(w3_ffn2[l]), bf(w2_ffn2[l]), gf,
                  l == depth - 1)
    return x2.reshape(b, s, d)
```

```python
import functools
import math

import numpy as np
import jax
import jax.numpy as jnp
from jax import lax
from jax.experimental import pallas as pl
from jax.experimental.pallas import tpu as pltpu

F32 = jnp.float32
BF16 = jnp.bfloat16

D_MODEL = 1024
SB_HEADS = 4
SB_HEAD_DIM = 64
RET_HEADS = 4
RET_QK_DIM = 64
RET_V_DIM = 128
ROPE_BASE = 10000.0
POOL_GROUPS = 4
POOL_GROUP_DIM = 64
POOL_WINDOWS = (2, 4, 8, 16)
N_BRANCHES = 3
D_FF = 2816
EPS = 1e-6

SB_W = SB_HEADS * SB_HEAD_DIM
RET_QK_W = RET_HEADS * RET_QK_DIM
RET_V_W = RET_HEADS * RET_V_DIM
POOL_W = POOL_GROUPS * POOL_GROUP_DIM
D_IN = SB_W * 3 + RET_QK_W * 2 + RET_V_W * 2 + POOL_W + N_BRANCHES * D_MODEL

OFF_QSB = 0
OFF_KSB = OFF_QSB + SB_W
OFF_VSB = OFF_KSB + SB_W
OFF_QR = OFF_VSB + SB_W
OFF_KR = OFF_QR + RET_QK_W
OFF_VR = OFF_KR + RET_QK_W
OFF_GR = OFF_VR + RET_V_W
OFF_UP = OFF_GR + RET_V_W
OFF_GATE = OFF_UP + POOL_W

SUBLANES = 8
LANES = 128
MXU_DIM = 256
VMEM_LIMIT = 56 * 1024 * 1024

ROW_TILE = 512
FF_CHUNK = 256
SB_Q_TILE = 512
SB_K_TILE = 256
SB_STRANDS = SB_K_TILE // SUBLANES
RET_CHUNK = 256
POOL_HALO = 16


def _resident(shape):
    nd = len(shape)
    return pl.BlockSpec(shape, lambda *_: (0,) * nd, pipeline_mode=pl.Buffered(1))


def _rms(x, g):
    ms = jnp.mean(x * x, axis=-1, keepdims=True)
    return x * lax.rsqrt(ms + EPS) * g


def _ffn_body(x_ref, g_ref, w1_ref, w3_ref, w2_ref, gf_ref, o_ref, acc_ref, *, final_norm):
    x = x_ref[...]
    h = _rms(x, g_ref[...]).astype(BF16)
    for c in range(D_FF // FF_CHUNK):
        sl = slice(c * FF_CHUNK, (c + 1) * FF_CHUNK)
        a = jnp.dot(h, w1_ref[:, sl], preferred_element_type=F32)
        b = jnp.dot(h, w3_ref[:, sl], preferred_element_type=F32)
        act = (a * jax.nn.sigmoid(a) * b).astype(BF16)
        part = jnp.dot(act, w2_ref[sl, :], preferred_element_type=F32)
        if c == 0:
            acc_ref[...] = part
        else:
            acc_ref[...] += part
    y = x + 0.5 * acc_ref[...]
    if final_norm:
        y = _rms(y, gf_ref[...])
    o_ref[...] = y


def _ffn(x2, g, w1, w3, w2, g_final, final_norm):
    t = x2.shape[0]
    tm = ROW_TILE
    row = pl.BlockSpec((tm, D_MODEL), lambda i: (i, 0))
    return pl.pallas_call(
        functools.partial(_ffn_body, final_norm=final_norm),
        grid=(t // tm,),
        in_specs=[row, _resident((1, D_MODEL)), _resident((D_MODEL, D_FF)), _resident((D_MODEL, D_FF)),
                  _resident((D_FF, D_MODEL)), _resident((1, D_MODEL))],
        out_specs=row,
        out_shape=jax.ShapeDtypeStruct((t, D_MODEL), F32),
        scratch_shapes=[pltpu.VMEM((tm, D_MODEL), F32)],
        compiler_params=pltpu.CompilerParams(dimension_semantics=("parallel",), vmem_limit_bytes=VMEM_LIMIT),
        name="ffn_half",
    )(x2, g, w1, w3, w2, g_final)


def _proj_body(x_ref, g_ref, w_ref, qsb_ref, ksb_ref, vsb_ref, qr_ref, kr_ref, vr_ref, gr_ref, up_ref, gate_ref,
               perm_scr):
    h = _rms(x_ref[...], g_ref[...]).astype(BF16)

    def mm(lo, width):
        return jnp.dot(h, w_ref[:, lo:lo + width], preferred_element_type=F32)

    def stage(vals):
        for c in range(SB_W // LANES):
            perm_scr[c] = vals[:, c * LANES:(c + 1) * LANES]

    def strand_major(kt):
        base = kt * SB_K_TILE
        return jnp.concatenate(
            [jnp.concatenate([perm_scr[c, pl.ds(base + i, SUBLANES, stride=SB_STRANDS), :]
                              for i in range(SB_STRANDS)], axis=0) for c in range(SB_W // LANES)], axis=1)

    qsb_ref[0] = (mm(OFF_QSB, SB_W) * (SB_HEAD_DIM ** -0.5 * 0.5)).T.astype(BF16)
    stage(mm(OFF_KSB, SB_W))
    for kt in range(ROW_TILE // SB_K_TILE):
        k_tile = strand_major(kt).astype(BF16)
        for hd in range(SB_HEADS):
            ksb_ref[0, hd, kt] = k_tile[:, hd * SB_HEAD_DIM:(hd + 1) * SB_HEAD_DIM]
    stage(mm(OFF_VSB, SB_W))
    for kt in range(ROW_TILE // SB_K_TILE):
        vsb_ref[0, kt] = strand_major(kt).T.astype(BF16)
    qr_ref[...] = mm(OFF_QR, RET_QK_W)
    kr_ref[...] = mm(OFF_KR, RET_QK_W)
    vr_ref[...] = mm(OFF_VR, RET_V_W).astype(BF16)
    gr_ref[...] = mm(OFF_GR, RET_V_W)
    up_ref[...] = mm(OFF_UP, POOL_W)
    for c in range(N_BRANCHES):
        gate_ref[:, c * D_MODEL:(c + 1) * D_MODEL] = mm(OFF_GATE + c * D_MODEL, D_MODEL)


def _proj(x2, g, w_in, b, s):
    t = x2.shape[0]
    tm = ROW_TILE
    tps = s // tm
    kt_per_tile = tm // SB_K_TILE
    nkt = s // SB_K_TILE

    def row(width):
        return pl.BlockSpec((tm, width), lambda i: (i, 0))

    key_tiles = pl.BlockSpec((1, SB_HEADS, kt_per_tile, SB_K_TILE, SB_HEAD_DIM),
                             lambda i: (i // tps, 0, i % tps, 0, 0))
    val_tiles = pl.BlockSpec((1, kt_per_tile, SB_W, SB_K_TILE), lambda i: (i // tps, i % tps, 0, 0))
    widths = (RET_QK_W, RET_QK_W, RET_V_W, RET_V_W, POOL_W, N_BRANCHES * D_MODEL)
    dtypes = (F32, F32, BF16, F32, F32, F32)
    return pl.pallas_call(
        _proj_body,
        grid=(t // tm,),
        in_specs=[row(D_MODEL), _resident((1, D_MODEL)), _resident((D_MODEL, D_IN))],
        out_specs=[pl.BlockSpec((1, SB_W, tm), lambda i: (i // tps, 0, i % tps)), key_tiles, val_tiles]
                  + [row(w) for w in widths],
        out_shape=[jax.ShapeDtypeStruct((b, SB_W, s), BF16),
                   jax.ShapeDtypeStruct((b, SB_HEADS, nkt, SB_K_TILE, SB_HEAD_DIM), BF16),
                   jax.ShapeDtypeStruct((b, nkt, SB_W, SB_K_TILE), BF16)]
                  + [jax.ShapeDtypeStruct((t, w), d) for w, d in zip(widths, dtypes)],
        scratch_shapes=[pltpu.VMEM((SB_W // LANES, tm, LANES), F32)],
        compiler_params=pltpu.CompilerParams(dimension_semantics=("parallel",), vmem_limit_bytes=VMEM_LIMIT),
        name="mix_proj",
    )(x2, g, w_in)


def _sb_body(q_ref, k_ref, v_ref, d_ref, o_ref, z_scr, p_scr, w_scr, o_scr):
    qi = pl.program_id(2)
    tq = SB_Q_TILE
    n_tiles = (qi + 1) * (tq // SB_K_TILE)
    o_scr[...] = jnp.zeros_like(o_scr)
    q_t = q_ref[0]
    sub = lax.broadcasted_iota(jnp.int32, (SUBLANES, tq), 0)

    def logits(j, slot):
        z_scr[slot] = jnp.dot(k_ref[0, 0, j], q_t, preferred_element_type=F32)

    def weights(j, slot, carry, masked):
        thr = qi * tq - j * SB_K_TILE
        acc = jnp.ones((SUBLANES, tq), F32)
        for i in range(SB_STRANDS - 1, -1, -1):
            rows = slice(i * SUBLANES, (i + 1) * SUBLANES)
            th = jnp.tanh(z_scr[slot, rows, :])
            if masked:
                th = jnp.where(d_ref[rows, :] < thr, th, -1.0)
            half = 0.5 * th
            p_scr[rows, :] = (0.5 + half) * acc
            acc = acc * (0.5 - half)
        y = acc
        for sh in (1, 2, 4):
            y = y * jnp.where(sub + sh < SUBLANES, pltpu.roll(y, SUBLANES - sh, 0), 1.0)
        off = jnp.where(sub + 1 < SUBLANES, pltpu.roll(y, SUBLANES - 1, 0), 1.0) * carry
        off2 = jnp.concatenate([off, off], axis=0)
        for i in range(SB_K_TILE // (2 * SUBLANES)):
            rows = slice(i * 2 * SUBLANES, (i + 1) * 2 * SUBLANES)
            w_scr[slot, rows, :] = (p_scr[rows, :] * off2).astype(BF16)
        return carry * jnp.broadcast_to(y[0:1, :], (SUBLANES, tq))

    def values(j, slot):
        o_scr[...] += jnp.dot(v_ref[0, j], w_scr[slot], preferred_element_type=F32)

    top = n_tiles - 1
    carry = jnp.ones((SUBLANES, tq), F32)
    logits(top, 0)
    logits(top - 1, 1)
    carry = weights(top, 0, carry, True)
    logits(jnp.maximum(top - 2, 0), 0)
    carry = weights(top - 1, 1, carry, True)
    values(top, 0)

    def pair(state):
        p, carry, _ = state
        j = top - 2 * p
        logits(j - 1, 1)
        carry = weights(j, 0, carry, False)
        alive = jnp.max(carry[0:1, :]) > 0.0
        values(j + 1, 1)
        logits(jnp.maximum(j - 2, 0), 0)
        carry = weights(j - 1, 1, carry, False)
        values(j, 0)
        return p + 1, carry, alive

    n_pairs = n_tiles // 2
    p_end, _, _ = lax.while_loop(lambda st: (st[0] < n_pairs) & st[2], pair,
                                 (jnp.int32(1), carry, jnp.bool_(True)))
    values(top - 2 * p_end + 1, 1)
    o_ref[0] = o_scr[...]


def _sb_attention(q_t, k_p, v_t):
    b, _, s = q_t.shape
    dh = SB_HEAD_DIM
    nkt = s // SB_K_TILE
    tq = SB_Q_TILE
    r = np.arange(SB_K_TILE)
    key_in_tile = (r % SUBLANES) * SB_STRANDS + r // SUBLANES
    dmat = jnp.asarray(key_in_tile[:, None] - np.arange(tq)[None, :], dtype=jnp.int32)
    return pl.pallas_call(
        _sb_body,
        grid=(b, SB_HEADS, s // tq),
        in_specs=[
            pl.BlockSpec((1, dh, tq), lambda bi, hi, qi: (bi, hi, qi)),
            pl.BlockSpec((1, 1, nkt, SB_K_TILE, dh), lambda bi, hi, qi: (bi, hi, 0, 0, 0)),
            pl.BlockSpec((1, nkt, dh, SB_K_TILE), lambda bi, hi, qi: (bi, 0, hi, 0)),
            _resident((SB_K_TILE, tq)),
        ],
        out_specs=pl.BlockSpec((1, dh, tq), lambda bi, hi, qi: (bi, hi, qi)),
        out_shape=jax.ShapeDtypeStruct((b, SB_W, s), F32),
        scratch_shapes=[pltpu.VMEM((2, SB_K_TILE, tq), F32), pltpu.VMEM((SB_K_TILE, tq), F32), pltpu.VMEM((2, SB_K_TILE, tq), BF16),
                        pltpu.VMEM((dh, tq), F32)],
        compiler_params=pltpu.CompilerParams(dimension_semantics=("parallel", "parallel", "arbitrary"),
                                             vmem_limit_bytes=VMEM_LIMIT),
        name="stick_breaking",
    )(q_t, k_p, v_t, dmat)


def _ret_log_gamma(h):
    return math.log(1.0 - 2.0 ** (-5.0 - h))


def _ret_body(q_ref, k_ref, v_ref, g_ref, cos_ref, sa_ref, sb_ref, dec_ref, qd_ref, kd_ref, o_ref, st_scr):
    c = RET_CHUNK

    @pl.when(pl.program_id(1) == 0)
    def _():
        st_scr[...] = jnp.zeros_like(st_scr)

    cos = cos_ref[...]
    sa = sa_ref[...]
    sb = sb_ref[...]

    def rot(x):
        return x * cos + pltpu.roll(x, RET_QK_DIM // 2, 1) * sa + pltpu.roll(x, RET_QK_W - RET_QK_DIM // 2, 1) * sb

    q = rot(q_ref[...])
    k = rot(k_ref[...])
    kb = k.astype(BF16)
    qdb = (q * qd_ref[...]).astype(BF16)
    kdt = (k * kd_ref[...]).T.astype(BF16)
    v = v_ref[...]
    lane = lax.broadcasted_iota(jnp.int32, (c, RET_QK_W), 1)
    st = st_scr[...]
    stb = st.astype(BF16)
    kv_all = jnp.dot(kdt, v, preferred_element_type=F32)

    for h in range(RET_HEADS):
        head = (lane >= h * RET_QK_DIM) & (lane < (h + 1) * RET_QK_DIM)
        qh = jnp.where(head, q, 0.0).astype(BF16)
        scores = lax.dot_general(qh, kb, (((1,), (1,)), ((), ())), preferred_element_type=F32)
        scores = (scores * dec_ref[h]).astype(BF16)
        vh = v[:, h * RET_V_DIM:(h + 1) * RET_V_DIM]
        intra = jnp.dot(scores, vh, preferred_element_type=F32)
        qdh = jnp.where(head, qdb, jnp.zeros_like(qdb))
        cross = jnp.dot(qdh, stb, preferred_element_type=F32)
        y = intra + cross
        y = y * lax.rsqrt(jnp.mean(y * y, axis=-1, keepdims=True) + EPS)
        gh = g_ref[:, h * RET_V_DIM:(h + 1) * RET_V_DIM]
        o_ref[:, h * RET_V_DIM:(h + 1) * RET_V_DIM] = (y * (gh * jax.nn.sigmoid(gh))).astype(BF16)
        rows = slice(h * RET_QK_DIM, (h + 1) * RET_QK_DIM)
        st_scr[rows, :] = (math.exp(c * _ret_log_gamma(h)) * st[rows, :]
                           + kv_all[rows, h * RET_V_DIM:(h + 1) * RET_V_DIM])


def _ret_tables(s):
    half = RET_QK_DIM // 2
    pos = jnp.arange(s, dtype=F32)
    inv_freq = ROPE_BASE ** (-jnp.arange(half, dtype=F32) / half)
    ang = pos[:, None] * inv_freq[None, :]
    cos, sin = jnp.cos(ang), jnp.sin(ang)
    zero = jnp.zeros_like(sin)
    cos_t = jnp.tile(jnp.concatenate([cos, cos], axis=1), (1, RET_HEADS))
    sa_t = jnp.tile(jnp.concatenate([zero, sin], axis=1), (1, RET_HEADS))
    sb_t = jnp.tile(jnp.concatenate([-sin, zero], axis=1), (1, RET_HEADS))

    c = RET_CHUNK
    idx = np.arange(c, dtype=np.float64)
    lg = np.array([_ret_log_gamma(h) for h in range(RET_HEADS)])
    diff = idx[:, None] - idx[None, :]
    scale = RET_QK_DIM ** -0.5
    dec = np.where(diff >= 0, np.exp(np.maximum(diff, 0.0)[None] * lg[:, None, None]), 0.0) * scale
    qd = np.repeat(np.exp((idx + 1)[:, None] * lg[None, :]), RET_QK_DIM, axis=1)
    kd = np.repeat(np.exp((c - 1 - idx)[:, None] * lg[None, :]), RET_QK_DIM, axis=1) * scale
    return cos_t, sa_t, sb_t, jnp.asarray(dec, F32), jnp.asarray(qd, F32), jnp.asarray(kd, F32)


def _retention(q, k, v, g, tables, b, s):
    c = RET_CHUNK
    nc = s // c
    cos_t, sa_t, sb_t, dec, qd, kd = tables

    def row(width):
        return pl.BlockSpec((c, width), lambda bi, ci: (bi * nc + ci, 0))

    def pos(width):
        return pl.BlockSpec((c, width), lambda bi, ci: (ci, 0))

    return pl.pallas_call(
        _ret_body,
        grid=(b, nc),
        in_specs=[row(RET_QK_W), row(RET_QK_W), row(RET_V_W), row(RET_V_W),
                  pos(RET_QK_W), pos(RET_QK_W), pos(RET_QK_W),
                  _resident((RET_HEADS, c, c)), _resident((c, RET_QK_W)), _resident((c, RET_QK_W))],
        out_specs=row(RET_V_W),
        out_shape=jax.ShapeDtypeStruct((b * s, RET_V_W), BF16),
        scratch_shapes=[pltpu.VMEM((RET_QK_W, RET_V_DIM), F32)],
        compiler_params=pltpu.CompilerParams(dimension_semantics=("parallel", "arbitrary"),
                                             vmem_limit_bytes=VMEM_LIMIT),
        name="retention",
    )(q, k, v, g, cos_t, sa_t, sb_t, dec, qd, kd)


def _merge_body(x_ref, ysb_ref, yret_ref, u_ref, halo_ref, gate_ref, wsb_ref, wret_ref, wpool_ref, pscale_ref,
                wbp_ref, wout_ref, o_ref, *, tiles_per_seq):
    tm = ROW_TILE
    i = pl.program_id(0)
    first = (i % tiles_per_seq) == 0
    u = u_ref[...]
    halo = jnp.where(first, 0.0, halo_ref[...])
    ext = jnp.concatenate([halo, u], axis=0)
    s2 = ext + pltpu.roll(ext, 1, 0)
    s4 = s2 + pltpu.roll(s2, 2, 0)
    s8 = s4 + pltpu.roll(s4, 4, 0)
    s16 = s8 + pltpu.roll(s8, 8, 0)
    lane = lax.broadcasted_iota(jnp.int32, (tm, POOL_W), 1)
    grp = lane // POOL_GROUP_DIM
    h0 = POOL_HALO
    win_sum = jnp.where(grp == 0, s2[h0:], jnp.where(grp == 1, s4[h0:], jnp.where(grp == 2, s8[h0:], s16[h0:])))
    window = jnp.where(grp == 0, POOL_WINDOWS[0],
                       jnp.where(grp == 1, POOL_WINDOWS[1], jnp.where(grp == 2, POOL_WINDOWS[2], POOL_WINDOWS[3])))
    t_pos = (i % tiles_per_seq) * tm + lax.broadcasted_iota(jnp.int32, (tm, POOL_W), 0)
    count = jnp.minimum(t_pos + 1, window).astype(F32)
    pooled = (win_sum / count - u).astype(BF16)
    y_pool = jnp.dot(pooled, wpool_ref[...], preferred_element_type=F32) * pscale_ref[...]

    def gate(c):
        return jax.nn.sigmoid(gate_ref[:, c * D_MODEL:(c + 1) * D_MODEL])

    y_sb = ysb_ref[0].T.astype(BF16)
    merged = gate(0) * jnp.dot(y_sb, wsb_ref[...], preferred_element_type=F32)
    merged += gate(1) * jnp.dot(yret_ref[...], wret_ref[...], preferred_element_type=F32)
    merged += gate(2) * jnp.dot(y_pool.astype(BF16), wbp_ref[...], preferred_element_type=F32)
    o_ref[...] = x_ref[...] + jnp.dot(merged.astype(BF16), wout_ref[...], preferred_element_type=F32)


def _merge(x2, y_sb, y_ret, u, gate, wsb, wret, wpool_bd, pscale, wbp, wout, s):
    t = x2.shape[0]
    tm = ROW_TILE
    halo_per_tile = tm // POOL_HALO
    tps = s // tm

    def row(width):
        return pl.BlockSpec((tm, width), lambda i: (i, 0))

    halo = pl.BlockSpec((POOL_HALO, POOL_W), lambda i: (jnp.maximum(i * halo_per_tile - 1, 0), 0))
    return pl.pallas_call(
        functools.partial(_merge_body, tiles_per_seq=s // tm),
        grid=(t // tm,),
        in_specs=[row(D_MODEL), pl.BlockSpec((1, SB_W, tm), lambda i: (i // tps, 0, i % tps)),
                  row(RET_V_W), row(POOL_W), halo, row(N_BRANCHES * D_MODEL),
                  _resident((SB_W, D_MODEL)), _resident((RET_V_W, D_MODEL)), _resident((POOL_W, POOL_W)),
                  _resident((1, POOL_W)), _resident((POOL_W, D_MODEL)), _resident((D_MODEL, D_MODEL))],
        out_specs=row(D_MODEL),
        out_shape=jax.ShapeDtypeStruct((t, D_MODEL), F32),
        compiler_params=pltpu.CompilerParams(dimension_semantics=("parallel",), vmem_limit_bytes=VMEM_LIMIT),
        name="merge_out",
    )(x2, y_sb, y_ret, u, u, gate, wsb, wret, wpool_bd, pscale, wbp, wout)


def _block_diag(w_pool):
    out = jnp.zeros((POOL_W, POOL_W), w_pool.dtype)
    for g in range(POOL_GROUPS):
        sl = slice(g * POOL_GROUP_DIM, (g + 1) * POOL_GROUP_DIM)
        out = out.at[sl, sl].set(w_pool[g])
    return out


def kernel(x, g_ffn1, w1_ffn1, w3_ffn1, w2_ffn1, g_mix, w_in, w_branch_sb, w_branch_ret, w_branch_pool, w_pool,
           pool_scale, w_out, g_ffn2, w1_ffn2, w3_ffn2, w2_ffn2, g_final):
    b, s, d = x.shape
    assert d == D_MODEL and s % SB_Q_TILE == 0 and s % RET_CHUNK == 0 and s % ROW_TILE == 0
    depth = g_ffn1.shape[0]
    t = b * s
    x2 = x.reshape(t, d)
    tables = _ret_tables(s)
    gf = g_final.reshape(1, d)

    def bf(w):
        return w.astype(BF16)

    for l in range(depth):
        x2 = _ffn(x2, g_ffn1[l].reshape(1, d), bf(w1_ffn1[l]), bf(w3_ffn1[l]), bf(w2_ffn1[l]), gf, False)

        q_t, k_p, v_t, q_r, k_r, v_r, g_r, u_p, gate = _proj(x2, g_mix[l].reshape(1, d), bf(w_in[l]), b, s)
        y_sb = _sb_attention(q_t, k_p, v_t)

        y_ret = _retention(q_r, k_r, v_r, g_r, tables, b, s)

        x2 = _merge(x2, y_sb, y_ret, u_p, gate, bf(w_branch_sb[l]), bf(w_branch_ret[l]), bf(_block_diag(w_pool[l])),
                    pool_scale[l].reshape(1, POOL_W), bf(w_branch_pool[l]), bf(w_out[l]), s)

        x2 = _ffn(x2, g_ffn2[l].reshape(1, d), bf(w1_ffn2[l]), bf(w3_ffn2[l]), bf(w2_ffn2[l]), gf,
                  l == depth - 1)
    return x2.reshape(b, s, d)
```

```python
import functools
import math

import numpy as np
import jax
import jax.numpy as jnp
from jax import lax
from jax.experimental import pallas as pl
from jax.experimental.pallas import tpu as pltpu

F32 = jnp.float32
BF16 = jnp.bfloat16

D_MODEL = 1024
SB_HEADS = 4
SB_HEAD_DIM = 64
RET_HEADS = 4
RET_QK_DIM = 64
RET_V_DIM = 128
ROPE_BASE = 10000.0
POOL_GROUPS = 4
POOL_GROUP_DIM = 64
POOL_WINDOWS = (2, 4, 8, 16)
N_BRANCHES = 3
D_FF = 2816
EPS = 1e-6

SB_W = SB_HEADS * SB_HEAD_DIM
RET_QK_W = RET_HEADS * RET_QK_DIM
RET_V_W = RET_HEADS * RET_V_DIM
POOL_W = POOL_GROUPS * POOL_GROUP_DIM
D_IN = SB_W * 3 + RET_QK_W * 2 + RET_V_W * 2 + POOL_W + N_BRANCHES * D_MODEL

OFF_QSB = 0
OFF_KSB = OFF_QSB + SB_W
OFF_VSB = OFF_KSB + SB_W
OFF_QR = OFF_VSB + SB_W
OFF_KR = OFF_QR + RET_QK_W
OFF_VR = OFF_KR + RET_QK_W
OFF_GR = OFF_VR + RET_V_W
OFF_UP = OFF_GR + RET_V_W
OFF_GATE = OFF_UP + POOL_W

SUBLANES = 8
LANES = 128
MXU_DIM = 256
VMEM_LIMIT = 56 * 1024 * 1024

ROW_TILE = 512
FFN_ROW_TILE = 1024
FF_CHUNK = 256
SB_Q_TILE = 1024
SB_K_TILE = 256
SB_STRANDS = SB_K_TILE // SUBLANES
RET_CHUNK = 256
RET_STEP_CHUNKS = 2
POOL_HALO = 16


def _resident(shape):
    nd = len(shape)
    return pl.BlockSpec(shape, lambda *_: (0,) * nd, pipeline_mode=pl.Buffered(1))


def _layer(l, shape):
    nd = len(shape)
    return pl.BlockSpec((None,) + tuple(shape), lambda *_: (l,) + (0,) * nd, pipeline_mode=pl.Buffered(1))


def _rms(x, g):
    ms = jnp.mean(x * x, axis=-1, keepdims=True)
    return x * lax.rsqrt(ms + EPS) * g


def _ffn_body(x_ref, g_ref, w1_ref, w3_ref, w2_ref, gf_ref, o_ref, acc_ref, *, final_norm):
    x = x_ref[...]
    h = _rms(x, g_ref[...]).astype(BF16)
    for c in range(D_FF // FF_CHUNK):
        sl = slice(c * FF_CHUNK, (c + 1) * FF_CHUNK)
        a = jnp.dot(h, w1_ref[:, sl], preferred_element_type=F32)
        b = jnp.dot(h, w3_ref[:, sl], preferred_element_type=F32)
        act = (a * jax.nn.sigmoid(a) * b).astype(BF16)
        part = jnp.dot(act, w2_ref[sl, :], preferred_element_type=F32)
        if c == 0:
            acc_ref[...] = part
        else:
            acc_ref[...] += part
    y = x + 0.5 * acc_ref[...]
    if final_norm:
        y = _rms(y, gf_ref[...])
    o_ref[...] = y


def _ffn(x2, g, w1, w3, w2, g_final, final_norm, l):
    t = x2.shape[0]
    tm = FFN_ROW_TILE
    row = pl.BlockSpec((tm, D_MODEL), lambda i: (i, 0))
    return pl.pallas_call(
        functools.partial(_ffn_body, final_norm=final_norm),
        grid=(t // tm,),
        in_specs=[row, _resident((1, D_MODEL)), _layer(l, (D_MODEL, D_FF)), _layer(l, (D_MODEL, D_FF)),
                  _layer(l, (D_FF, D_MODEL)), _resident((1, D_MODEL))],
        out_specs=row,
        out_shape=jax.ShapeDtypeStruct((t, D_MODEL), F32),
        scratch_shapes=[pltpu.VMEM((tm, D_MODEL), F32)],
        compiler_params=pltpu.CompilerParams(dimension_semantics=("parallel",), vmem_limit_bytes=VMEM_LIMIT),
        name="ffn_half",
    )(x2, g, w1, w3, w2, g_final)


def _proj_body(x_ref, g_ref, w_ref, qsb_ref, ksb_ref, vsb_ref, qr_ref, kr_ref, vr_ref, gr_ref, up_ref, gate_ref,
               perm_scr):
    h = _rms(x_ref[...], g_ref[...]).astype(BF16)

    def mm(lo, width):
        return jnp.dot(h, w_ref[:, lo:lo + width], preferred_element_type=F32)

    def stage(vals):
        for c in range(SB_W // LANES):
            perm_scr[c] = vals[:, c * LANES:(c + 1) * LANES]

    def strand_major(kt):
        base = kt * SB_K_TILE
        return jnp.concatenate(
            [jnp.concatenate([perm_scr[c, pl.ds(base + i, SUBLANES, stride=SB_STRANDS), :]
                              for i in range(SB_STRANDS)], axis=0) for c in range(SB_W // LANES)], axis=1)

    qsb_ref[0] = (mm(OFF_QSB, SB_W) * (SB_HEAD_DIM ** -0.5 * 0.5)).T.astype(BF16)
    stage(mm(OFF_KSB, SB_W))
    for kt in range(ROW_TILE // SB_K_TILE):
        k_tile = strand_major(kt).astype(BF16)
        for hd in range(SB_HEADS):
            ksb_ref[0, hd, kt] = k_tile[:, hd * SB_HEAD_DIM:(hd + 1) * SB_HEAD_DIM]
    stage(mm(OFF_VSB, SB_W))
    for kt in range(ROW_TILE // SB_K_TILE):
        vsb_ref[0, kt] = strand_major(kt).T.astype(BF16)
    qr_ref[...] = mm(OFF_QR, RET_QK_W)
    kr_ref[...] = mm(OFF_KR, RET_QK_W)
    vr_ref[...] = mm(OFF_VR, RET_V_W).astype(BF16)
    gr_ref[...] = mm(OFF_GR, RET_V_W)
    up_ref[...] = mm(OFF_UP, POOL_W)
    for c in range(N_BRANCHES):
        gate_ref[:, c * D_MODEL:(c + 1) * D_MODEL] = mm(OFF_GATE + c * D_MODEL, D_MODEL).astype(BF16)


def _proj(x2, g, w_in, b, s, l):
    t = x2.shape[0]
    tm = ROW_TILE
    tps = s // tm
    kt_per_tile = tm // SB_K_TILE
    nkt = s // SB_K_TILE

    def row(width):
        return pl.BlockSpec((tm, width), lambda i: (i, 0))

    key_tiles = pl.BlockSpec((1, SB_HEADS, kt_per_tile, SB_K_TILE, SB_HEAD_DIM),
                             lambda i: (i // tps, 0, i % tps, 0, 0))
    val_tiles = pl.BlockSpec((1, kt_per_tile, SB_W, SB_K_TILE), lambda i: (i // tps, i % tps, 0, 0))
    widths = (RET_QK_W, RET_QK_W, RET_V_W, RET_V_W, POOL_W, N_BRANCHES * D_MODEL)
    dtypes = (F32, F32, BF16, F32, F32, BF16)
    return pl.pallas_call(
        _proj_body,
        grid=(t // tm,),
        in_specs=[row(D_MODEL), _resident((1, D_MODEL)), _layer(l, (D_MODEL, D_IN))],
        out_specs=[pl.BlockSpec((1, SB_W, tm), lambda i: (i // tps, 0, i % tps)), key_tiles, val_tiles]
                  + [row(w) for w in widths],
        out_shape=[jax.ShapeDtypeStruct((b, SB_W, s), BF16),
                   jax.ShapeDtypeStruct((b, SB_HEADS, nkt, SB_K_TILE, SB_HEAD_DIM), BF16),
                   jax.ShapeDtypeStruct((b, nkt, SB_W, SB_K_TILE), BF16)]
                  + [jax.ShapeDtypeStruct((t, w), d) for w, d in zip(widths, dtypes)],
        scratch_shapes=[pltpu.VMEM((SB_W // LANES, tm, LANES), F32)],
        compiler_params=pltpu.CompilerParams(dimension_semantics=("parallel",), vmem_limit_bytes=VMEM_LIMIT),
        name="mix_proj",
    )(x2, g, w_in)


def _sb_body(q_ref, k_ref, v_ref, d_ref, o_ref, z_scr, p_scr, w_scr, o_scr):
    qi = pl.program_id(2)
    tq = SB_Q_TILE
    n_diag = tq // SB_K_TILE
    n_tiles = (qi + 1) * n_diag
    o_scr[...] = jnp.zeros_like(o_scr)

    def logits(j, slot, lo=0, hi=tq):
        z_scr[slot, :, lo:hi] = jnp.dot(k_ref[0, 0, j], q_ref[0, :, lo:hi], preferred_element_type=F32)

    def weights(j, slot, carry, masked, lo=0, hi=tq):
        width = hi - lo
        sub = lax.broadcasted_iota(jnp.int32, (SUBLANES, width), 0)
        thr = qi * tq - j * SB_K_TILE
        acc = jnp.ones((SUBLANES, width), F32)
        for i in range(SB_STRANDS - 1, -1, -1):
            rows = slice(i * SUBLANES, (i + 1) * SUBLANES)
            th = jnp.tanh(z_scr[slot, rows, lo:hi])
            if masked:
                th = jnp.where(d_ref[rows, lo:hi] < thr, th, -1.0)
            half = 0.5 * th
            p_scr[rows, lo:hi] = (0.5 + half) * acc
            acc = acc * (0.5 - half)
        y = acc
        for sh in (1, 2, 4):
            y = y * jnp.where(sub + sh < SUBLANES, pltpu.roll(y, SUBLANES - sh, 0), 1.0)
        off = jnp.where(sub + 1 < SUBLANES, pltpu.roll(y, SUBLANES - 1, 0), 1.0) * carry[:, lo:hi]
        off2 = jnp.concatenate([off, off], axis=0)
        for i in range(SB_K_TILE // (2 * SUBLANES)):
            rows = slice(i * 2 * SUBLANES, (i + 1) * 2 * SUBLANES)
            w_scr[slot, rows, lo:hi] = (p_scr[rows, lo:hi] * off2).astype(BF16)
        new = carry[:, lo:hi] * jnp.broadcast_to(y[0:1, :], (SUBLANES, width))
        parts = ([carry[:, :lo]] if lo > 0 else []) + [new] + ([carry[:, hi:]] if hi < tq else [])
        return new if len(parts) == 1 else jnp.concatenate(parts, axis=1)

    def values(j, slot, lo=0, hi=tq):
        o_scr[:, lo:hi] += jnp.dot(v_ref[0, j], w_scr[slot, :, lo:hi], preferred_element_type=F32)

    def first_lane(m):
        return (n_diag - 1 - m) * SB_K_TILE

    top = n_tiles - 1
    carry = jnp.ones((SUBLANES, tq), F32)
    logits(top, 0, first_lane(0))
    for p in range(n_diag // 2):
        m = 2 * p
        logits(top - m - 1, 1, first_lane(m + 1))
        carry = weights(top - m, 0, carry, True, first_lane(m))
        if p > 0:
            values(top - m + 1, 1, first_lane(m - 1))
        logits(jnp.maximum(top - m - 2, 0), 0, first_lane(m + 2) if m + 2 < n_diag else 0)
        carry = weights(top - m - 1, 1, carry, True, first_lane(m + 1))
        values(top - m, 0, first_lane(m))
    values(top - n_diag + 1, 1)

    def pair(state, lo, hi):
        p, carry, _ = state
        j = top - 2 * p
        logits(j - 1, 1, lo, hi)
        carry = weights(j, 0, carry, False, lo, hi)
        alive = jnp.max(carry[0:1, lo:hi]) > 0.0
        values(j, 0, lo, hi)
        logits(jnp.maximum(j - 2, 0), 0, lo, hi)
        carry = weights(j - 1, 1, carry, False, lo, hi)
        values(j - 1, 1, lo, hi)
        return p + 1, carry, alive

    n_pairs = n_tiles // 2
    narrow = jnp.max(carry[0:1, SB_K_TILE:]) == 0.0
    state = (jnp.int32(n_diag // 2), carry, jnp.bool_(True))
    state = lax.while_loop(lambda st: (st[0] < n_pairs) & st[2] & narrow,
                           functools.partial(pair, lo=0, hi=SB_K_TILE), state)
    lax.while_loop(lambda st: (st[0] < n_pairs) & st[2], functools.partial(pair, lo=0, hi=tq), state)
    o_ref[0] = o_scr[...]


def _sb_attention(q_t, k_p, v_t):
    b, _, s = q_t.shape
    dh = SB_HEAD_DIM
    nkt = s // SB_K_TILE
    tq = SB_Q_TILE
    r = np.arange(SB_K_TILE)
    key_in_tile = (r % SUBLANES) * SB_STRANDS + r // SUBLANES
    dmat = jnp.asarray(key_in_tile[:, None] - np.arange(tq)[None, :], dtype=jnp.int32)
    return pl.pallas_call(
        _sb_body,
        grid=(b, SB_HEADS, s // tq),
        in_specs=[
            pl.BlockSpec((1, dh, tq), lambda bi, hi, qi: (bi, hi, qi)),
            pl.BlockSpec((1, 1, nkt, SB_K_TILE, dh), lambda bi, hi, qi: (bi, hi, 0, 0, 0)),
            pl.BlockSpec((1, nkt, dh, SB_K_TILE), lambda bi, hi, qi: (bi, 0, hi, 0)),
            _resident((SB_K_TILE, tq)),
        ],
        out_specs=pl.BlockSpec((1, dh, tq), lambda bi, hi, qi: (bi, hi, qi)),
        out_shape=jax.ShapeDtypeStruct((b, SB_W, s), F32),
        scratch_shapes=[pltpu.VMEM((2, SB_K_TILE, tq), F32), pltpu.VMEM((SB_K_TILE, tq), F32), pltpu.VMEM((2, SB_K_TILE, tq), BF16),
                        pltpu.VMEM((dh, tq), F32)],
        compiler_params=pltpu.CompilerParams(dimension_semantics=("parallel", "parallel", "arbitrary"),
                                             vmem_limit_bytes=VMEM_LIMIT),
        name="stick_breaking",
    )(q_t, k_p, v_t, dmat)


def _ret_log_gamma(h):
    return math.log(1.0 - 2.0 ** (-5.0 - h))


def _ret_body(q_ref, k_ref, v_ref, g_ref, cos_ref, sa_ref, sb_ref, dec_ref, qd_ref, kd_ref, o_ref, st_scr):
    c = RET_CHUNK

    @pl.when(pl.program_id(0) == 0)
    def _():
        st_scr[...] = jnp.zeros_like(st_scr)

    lane = lax.broadcasted_iota(jnp.int32, (c, RET_QK_W), 1)

    for bi in range(q_ref.shape[0]):
        st = st_scr[bi]
        for ci in range(RET_STEP_CHUNKS):
            rs = slice(ci * c, (ci + 1) * c)
            cos, sa, sb = cos_ref[rs, :], sa_ref[rs, :], sb_ref[rs, :]

            def rot(x):
                return (x * cos + pltpu.roll(x, RET_QK_DIM // 2, 1) * sa
                        + pltpu.roll(x, RET_QK_W - RET_QK_DIM // 2, 1) * sb)

            q = rot(q_ref[bi, rs, :])
            k = rot(k_ref[bi, rs, :])
            kb = k.astype(BF16)
            qdb = (q * qd_ref[...]).astype(BF16)
            kdt = (k * kd_ref[...]).T.astype(BF16)
            v = v_ref[bi, rs, :]
            stb = st.astype(BF16)
            kv_all = jnp.dot(kdt, v, preferred_element_type=F32)
            new_st = []
            for h in range(RET_HEADS):
                head = (lane >= h * RET_QK_DIM) & (lane < (h + 1) * RET_QK_DIM)
                qh = jnp.where(head, q, 0.0).astype(BF16)
                scores = lax.dot_general(qh, kb, (((1,), (1,)), ((), ())), preferred_element_type=F32)
                scores = (scores * dec_ref[h]).astype(BF16)
                vh = v[:, h * RET_V_DIM:(h + 1) * RET_V_DIM]
                intra = jnp.dot(scores, vh, preferred_element_type=F32)
                qdh = jnp.where(head, qdb, jnp.zeros_like(qdb))
                cross = jnp.dot(qdh, stb, preferred_element_type=F32)
                y = intra + cross
                y = y * lax.rsqrt(jnp.mean(y * y, axis=-1, keepdims=True) + EPS)
                gh = g_ref[bi, rs, h * RET_V_DIM:(h + 1) * RET_V_DIM]
                o_ref[bi, rs, h * RET_V_DIM:(h + 1) * RET_V_DIM] = (y * (gh * jax.nn.sigmoid(gh))).astype(BF16)
                rows = slice(h * RET_QK_DIM, (h + 1) * RET_QK_DIM)
                new_st.append(math.exp(c * _ret_log_gamma(h)) * st[rows, :]
                              + kv_all[rows, h * RET_V_DIM:(h + 1) * RET_V_DIM])
            st = jnp.concatenate(new_st, axis=0)
        st_scr[bi] = st


def _ret_tables(s):
    half = RET_QK_DIM // 2
    pos = jnp.arange(s, dtype=F32)
    inv_freq = ROPE_BASE ** (-jnp.arange(half, dtype=F32) / half)
    ang = pos[:, None] * inv_freq[None, :]
    cos, sin = jnp.cos(ang), jnp.sin(ang)
    zero = jnp.zeros_like(sin)
    cos_t = jnp.tile(jnp.concatenate([cos, cos], axis=1), (1, RET_HEADS))
    sa_t = jnp.tile(jnp.concatenate([zero, sin], axis=1), (1, RET_HEADS))
    sb_t = jnp.tile(jnp.concatenate([-sin, zero], axis=1), (1, RET_HEADS))

    c = RET_CHUNK
    idx = np.arange(c, dtype=np.float64)
    lg = np.array([_ret_log_gamma(h) for h in range(RET_HEADS)])
    diff = idx[:, None] - idx[None, :]
    scale = RET_QK_DIM ** -0.5
    dec = np.where(diff >= 0, np.exp(np.maximum(diff, 0.0)[None] * lg[:, None, None]), 0.0) * scale
    qd = np.repeat(np.exp((idx + 1)[:, None] * lg[None, :]), RET_QK_DIM, axis=1)
    kd = np.repeat(np.exp((c - 1 - idx)[:, None] * lg[None, :]), RET_QK_DIM, axis=1) * scale
    return cos_t, sa_t, sb_t, jnp.asarray(dec, F32), jnp.asarray(qd, F32), jnp.asarray(kd, F32)


def _retention(q, k, v, g, tables, b, s):
    c = RET_CHUNK
    rows = c * RET_STEP_CHUNKS
    cos_t, sa_t, sb_t, dec, qd, kd = tables

    def row(width):
        return pl.BlockSpec((b, rows, width), lambda ci: (0, ci, 0))

    def pos(width):
        return pl.BlockSpec((rows, width), lambda ci: (ci, 0))

    def seq(a):
        return a.reshape(b, s, a.shape[-1])

    out = pl.pallas_call(
        _ret_body,
        grid=(s // rows,),
        in_specs=[row(RET_QK_W), row(RET_QK_W), row(RET_V_W), row(RET_V_W),
                  pos(RET_QK_W), pos(RET_QK_W), pos(RET_QK_W),
                  _resident((RET_HEADS, c, c)), _resident((c, RET_QK_W)), _resident((c, RET_QK_W))],
        out_specs=row(RET_V_W),
        out_shape=jax.ShapeDtypeStruct((b, s, RET_V_W), BF16),
        scratch_shapes=[pltpu.VMEM((b, RET_QK_W, RET_V_DIM), F32)],
        compiler_params=pltpu.CompilerParams(dimension_semantics=("arbitrary",), vmem_limit_bytes=VMEM_LIMIT),
        name="retention",
    )(seq(q), seq(k), seq(v), seq(g), cos_t, sa_t, sb_t, dec, qd, kd)
    return out.reshape(b * s, RET_V_W)


def _merge_body(x_ref, ysb_ref, yret_ref, u_ref, halo_ref, gate_ref, wsb_ref, wret_ref, wpool_ref, pscale_ref,
                wbp_ref, wout_ref, o_ref, *, tiles_per_seq):
    tm = ROW_TILE
    i = pl.program_id(0)
    first = (i % tiles_per_seq) == 0
    u = u_ref[...]
    halo = jnp.where(first, 0.0, halo_ref[...])
    ext = jnp.concatenate([halo, u], axis=0)
    s2 = ext + pltpu.roll(ext, 1, 0)
    s4 = s2 + pltpu.roll(s2, 2, 0)
    s8 = s4 + pltpu.roll(s4, 4, 0)
    s16 = s8 + pltpu.roll(s8, 8, 0)
    lane = lax.broadcasted_iota(jnp.int32, (tm, POOL_W), 1)
    grp = lane // POOL_GROUP_DIM
    h0 = POOL_HALO
    win_sum = jnp.where(grp == 0, s2[h0:], jnp.where(grp == 1, s4[h0:], jnp.where(grp == 2, s8[h0:], s16[h0:])))
    window = jnp.where(grp == 0, POOL_WINDOWS[0],
                       jnp.where(grp == 1, POOL_WINDOWS[1], jnp.where(grp == 2, POOL_WINDOWS[2], POOL_WINDOWS[3])))
    t_pos = (i % tiles_per_seq) * tm + lax.broadcasted_iota(jnp.int32, (tm, POOL_W), 0)
    count = jnp.minimum(t_pos + 1, window).astype(F32)
    pooled = (win_sum / count - u).astype(BF16)
    y_pool = jnp.dot(pooled, wpool_ref[...], preferred_element_type=F32) * pscale_ref[...]

    def gate(c):
        return 0.5 + 0.5 * jnp.tanh(0.5 * gate_ref[:, c * D_MODEL:(c + 1) * D_MODEL].astype(F32))

    y_sb = ysb_ref[0].T.astype(BF16)
    merged = gate(0) * jnp.dot(y_sb, wsb_ref[...], preferred_element_type=F32)
    merged += gate(1) * jnp.dot(yret_ref[...], wret_ref[...], preferred_element_type=F32)
    merged += gate(2) * jnp.dot(y_pool.astype(BF16), wbp_ref[...], preferred_element_type=F32)
    o_ref[...] = x_ref[...] + jnp.dot(merged.astype(BF16), wout_ref[...], preferred_element_type=F32)


def _merge(x2, y_sb, y_ret, u, gate, wsb, wret, wpool_bd, pscale, wbp, wout, s, l):
    t = x2.shape[0]
    tm = ROW_TILE
    halo_per_tile = tm // POOL_HALO
    tps = s // tm

    def row(width):
        return pl.BlockSpec((tm, width), lambda i: (i, 0))

    halo = pl.BlockSpec((POOL_HALO, POOL_W), lambda i: (jnp.maximum(i * halo_per_tile - 1, 0), 0))
    return pl.pallas_call(
        functools.partial(_merge_body, tiles_per_seq=s // tm),
        grid=(t // tm,),
        in_specs=[row(D_MODEL), pl.BlockSpec((1, SB_W, tm), lambda i: (i // tps, 0, i % tps)),
                  row(RET_V_W), row(POOL_W), halo, row(N_BRANCHES * D_MODEL),
                  _layer(l, (SB_W, D_MODEL)), _layer(l, (RET_V_W, D_MODEL)), _resident((POOL_W, POOL_W)),
                  _resident((1, POOL_W)), _layer(l, (POOL_W, D_MODEL)), _layer(l, (D_MODEL, D_MODEL))],
        out_specs=row(D_MODEL),
        out_shape=jax.ShapeDtypeStruct((t, D_MODEL), F32),
        compiler_params=pltpu.CompilerParams(dimension_semantics=("parallel",), vmem_limit_bytes=VMEM_LIMIT),
        name="merge_out",
    )(x2, y_sb, y_ret, u, u, gate, wsb, wret, wpool_bd, pscale, wbp, wout)


def _block_diag(w_pool):
    out = jnp.zeros((POOL_W, POOL_W), w_pool.dtype)
    for g in range(POOL_GROUPS):
        sl = slice(g * POOL_GROUP_DIM, (g + 1) * POOL_GROUP_DIM)
        out = out.at[sl, sl].set(w_pool[g])
    return out


def kernel(x, g_ffn1, w1_ffn1, w3_ffn1, w2_ffn1, g_mix, w_in, w_branch_sb, w_branch_ret, w_branch_pool, w_pool,
           pool_scale, w_out, g_ffn2, w1_ffn2, w3_ffn2, w2_ffn2, g_final):
    b, s, d = x.shape
    assert d == D_MODEL and s % SB_Q_TILE == 0 and s % (RET_CHUNK * RET_STEP_CHUNKS) == 0 and s % ROW_TILE == 0
    depth = g_ffn1.shape[0]
    t = b * s
    assert t % FFN_ROW_TILE == 0
    x2 = x.reshape(t, d)
    tables = _ret_tables(s)
    gf = g_final.reshape(1, d)

    def bf(w):
        return w.astype(BF16)

    w1a, w3a, w2a = bf(w1_ffn1), bf(w3_ffn1), bf(w2_ffn1)
    w1b, w3b, w2b = bf(w1_ffn2), bf(w3_ffn2), bf(w2_ffn2)
    w_in_b, w_out_b = bf(w_in), bf(w_out)
    w_sb, w_ret, w_bp = bf(w_branch_sb), bf(w_branch_ret), bf(w_branch_pool)

    for l in range(depth):
        x2 = _ffn(x2, g_ffn1[l].reshape(1, d), w1a, w3a, w2a, gf, False, l)

        q_t, k_p, v_t, q_r, k_r, v_r, g_r, u_p, gate = _proj(x2, g_mix[l].reshape(1, d), w_in_b, b, s, l)
        y_sb = _sb_attention(q_t, k_p, v_t)

        y_ret = _retention(q_r, k_r, v_r, g_r, tables, b, s)

        x2 = _merge(x2, y_sb, y_ret, u_p, gate, w_sb, w_ret, bf(_block_diag(w_pool[l])),
                    pool_scale[l].reshape(1, POOL_W), w_bp, w_out_b, s, l)

        x2 = _ffn(x2, g_ffn2[l].reshape(1, d), w1b, w3b, w2b, gf, l == depth - 1, l)
    return x2.reshape(b, s, d)
```

```python
import functools
import math

import numpy as np
import jax
import jax.numpy as jnp
from jax import lax
from jax.experimental import pallas as pl
from jax.experimental.pallas import tpu as pltpu

F32 = jnp.float32
BF16 = jnp.bfloat16

D_MODEL = 1024
SB_HEADS = 4
SB_HEAD_DIM = 64
RET_HEADS = 4
RET_QK_DIM = 64
RET_V_DIM = 128
ROPE_BASE = 10000.0
POOL_GROUPS = 4
POOL_GROUP_DIM = 64
POOL_WINDOWS = (2, 4, 8, 16)
N_BRANCHES = 3
D_FF = 2816
EPS = 1e-6

SB_W = SB_HEADS * SB_HEAD_DIM
RET_QK_W = RET_HEADS * RET_QK_DIM
RET_V_W = RET_HEADS * RET_V_DIM
POOL_W = POOL_GROUPS * POOL_GROUP_DIM
D_IN = SB_W * 3 + RET_QK_W * 2 + RET_V_W * 2 + POOL_W + N_BRANCHES * D_MODEL

OFF_QSB = 0
OFF_KSB = OFF_QSB + SB_W
OFF_VSB = OFF_KSB + SB_W
OFF_QR = OFF_VSB + SB_W
OFF_KR = OFF_QR + RET_QK_W
OFF_VR = OFF_KR + RET_QK_W
OFF_GR = OFF_VR + RET_V_W
OFF_UP = OFF_GR + RET_V_W
OFF_GATE = OFF_UP + POOL_W

SUBLANES = 8
LANES = 128
MXU_DIM = 256
VMEM_LIMIT = 56 * 1024 * 1024

ROW_TILE = 1024
FFN_ROW_TILE = 1024
FF_CHUNK = 256
SB_Q_TILE = 1024
SB_K_TILE = 256
SB_STRANDS = SB_K_TILE // SUBLANES
RET_CHUNK = 256
RET_STEP_CHUNKS = 2
POOL_HALO = 16


def _resident(shape):
    nd = len(shape)
    return pl.BlockSpec(shape, lambda *_: (0,) * nd, pipeline_mode=pl.Buffered(1))


def _layer(l, shape):
    nd = len(shape)
    return pl.BlockSpec((None,) + tuple(shape), lambda *_: (l,) + (0,) * nd, pipeline_mode=pl.Buffered(1))


def _rms(x, g):
    ms = jnp.mean(x * x, axis=-1, keepdims=True)
    return x * lax.rsqrt(ms + EPS) * g


def _ffn_body(x_ref, g_ref, w1_ref, w3_ref, w2_ref, gf_ref, o_ref, acc_ref, *, final_norm):
    x = x_ref[...]
    h = _rms(x, g_ref[...]).astype(BF16)
    for c in range(D_FF // FF_CHUNK):
        sl = slice(c * FF_CHUNK, (c + 1) * FF_CHUNK)
        a = jnp.dot(h, w1_ref[:, sl], preferred_element_type=F32)
        b = jnp.dot(h, w3_ref[:, sl], preferred_element_type=F32)
        act = (a * jax.nn.sigmoid(a) * b).astype(BF16)
        part = jnp.dot(act, w2_ref[sl, :], preferred_element_type=F32)
        if c == 0:
            acc_ref[...] = part
        else:
            acc_ref[...] += part
    y = x + 0.5 * acc_ref[...]
    if final_norm:
        y = _rms(y, gf_ref[...])
    o_ref[...] = y


def _ffn(x2, g, w1, w3, w2, g_final, final_norm, l):
    t = x2.shape[0]
    tm = FFN_ROW_TILE
    row = pl.BlockSpec((tm, D_MODEL), lambda i: (i, 0))
    return pl.pallas_call(
        functools.partial(_ffn_body, final_norm=final_norm),
        grid=(t // tm,),
        in_specs=[row, _resident((1, D_MODEL)), _layer(l, (D_MODEL, D_FF)), _layer(l, (D_MODEL, D_FF)),
                  _layer(l, (D_FF, D_MODEL)), _resident((1, D_MODEL))],
        out_specs=row,
        out_shape=jax.ShapeDtypeStruct((t, D_MODEL), F32),
        scratch_shapes=[pltpu.VMEM((tm, D_MODEL), F32)],
        compiler_params=pltpu.CompilerParams(dimension_semantics=("parallel",), vmem_limit_bytes=VMEM_LIMIT),
        name="ffn_half",
    )(x2, g, w1, w3, w2, g_final)


def _proj_body(x_ref, g_ref, w_ref, qsb_ref, ksb_ref, vsb_ref, qr_ref, kr_ref, vr_ref, gr_ref, up_ref, gate_ref,
               perm_scr):
    h = _rms(x_ref[...], g_ref[...]).astype(BF16)

    def mm(lo, width):
        return jnp.dot(h, w_ref[:, lo:lo + width], preferred_element_type=F32)

    def stage(vals):
        for c in range(SB_W // LANES):
            perm_scr[c] = vals[:, c * LANES:(c + 1) * LANES]

    def strand_major(kt):
        base = kt * SB_K_TILE
        return jnp.concatenate(
            [jnp.concatenate([perm_scr[c, pl.ds(base + i, SUBLANES, stride=SB_STRANDS), :]
                              for i in range(SB_STRANDS)], axis=0) for c in range(SB_W // LANES)], axis=1)

    qsb_ref[0] = (mm(OFF_QSB, SB_W) * (SB_HEAD_DIM ** -0.5 * 0.5)).T.astype(BF16)
    stage(mm(OFF_KSB, SB_W))
    for kt in range(ROW_TILE // SB_K_TILE):
        k_tile = strand_major(kt).astype(BF16)
        for hd in range(SB_HEADS):
            ksb_ref[0, hd, kt] = k_tile[:, hd * SB_HEAD_DIM:(hd + 1) * SB_HEAD_DIM]
    stage(mm(OFF_VSB, SB_W))
    for kt in range(ROW_TILE // SB_K_TILE):
        vsb_ref[0, kt] = strand_major(kt).T.astype(BF16)
    qr_ref[...] = mm(OFF_QR, RET_QK_W)
    kr_ref[...] = mm(OFF_KR, RET_QK_W)
    vr_ref[...] = mm(OFF_VR, RET_V_W).astype(BF16)
    gr_ref[...] = mm(OFF_GR, RET_V_W)
    up_ref[...] = mm(OFF_UP, POOL_W)
    for c in range(N_BRANCHES):
        gate_ref[:, c * D_MODEL:(c + 1) * D_MODEL] = (0.5 * mm(OFF_GATE + c * D_MODEL, D_MODEL)).astype(BF16)


def _proj(x2, g, w_in, b, s, l):
    t = x2.shape[0]
    tm = ROW_TILE
    tps = s // tm
    kt_per_tile = tm // SB_K_TILE
    nkt = s // SB_K_TILE

    def row(width):
        return pl.BlockSpec((tm, width), lambda i: (i, 0))

    key_tiles = pl.BlockSpec((1, SB_HEADS, kt_per_tile, SB_K_TILE, SB_HEAD_DIM),
                             lambda i: (i // tps, 0, i % tps, 0, 0))
    val_tiles = pl.BlockSpec((1, kt_per_tile, SB_W, SB_K_TILE), lambda i: (i // tps, i % tps, 0, 0))
    widths = (RET_QK_W, RET_QK_W, RET_V_W, RET_V_W, POOL_W, N_BRANCHES * D_MODEL)
    dtypes = (F32, F32, BF16, F32, F32, BF16)
    return pl.pallas_call(
        _proj_body,
        grid=(t // tm,),
        in_specs=[row(D_MODEL), _resident((1, D_MODEL)), _layer(l, (D_MODEL, D_IN))],
        out_specs=[pl.BlockSpec((1, SB_W, tm), lambda i: (i // tps, 0, i % tps)), key_tiles, val_tiles]
                  + [row(w) for w in widths],
        out_shape=[jax.ShapeDtypeStruct((b, SB_W, s), BF16),
                   jax.ShapeDtypeStruct((b, SB_HEADS, nkt, SB_K_TILE, SB_HEAD_DIM), BF16),
                   jax.ShapeDtypeStruct((b, nkt, SB_W, SB_K_TILE), BF16)]
                  + [jax.ShapeDtypeStruct((t, w), d) for w, d in zip(widths, dtypes)],
        scratch_shapes=[pltpu.VMEM((SB_W // LANES, tm, LANES), F32)],
        compiler_params=pltpu.CompilerParams(dimension_semantics=("parallel",), vmem_limit_bytes=VMEM_LIMIT),
        name="mix_proj",
    )(x2, g, w_in)


def _sb_body(q_ref, k_ref, v_ref, d_ref, o_ref, z_scr, p_scr, w_scr, o_scr):
    qi = pl.program_id(2)
    tq = SB_Q_TILE
    n_diag = tq // SB_K_TILE
    n_tiles = (qi + 1) * n_diag
    o_scr[...] = jnp.zeros_like(o_scr)

    def logits(j, slot, lo=0, hi=tq):
        z_scr[slot, :, lo:hi] = jnp.dot(k_ref[0, 0, j], q_ref[0, :, lo:hi], preferred_element_type=F32)

    def weights(j, slot, carry, masked, lo=0, hi=tq):
        width = hi - lo
        sub = lax.broadcasted_iota(jnp.int32, (SUBLANES, width), 0)
        thr = qi * tq - j * SB_K_TILE
        acc = jnp.ones((SUBLANES, width), F32)
        for i in range(SB_STRANDS - 1, -1, -1):
            rows = slice(i * SUBLANES, (i + 1) * SUBLANES)
            th = jnp.tanh(z_scr[slot, rows, lo:hi])
            if masked:
                th = jnp.where(d_ref[rows, lo:hi] < thr, th, -1.0)
            half = 0.5 * th
            p_scr[rows, lo:hi] = (0.5 + half) * acc
            acc = acc * (0.5 - half)
        y = acc
        for sh in (1, 2, 4):
            y = y * jnp.where(sub + sh < SUBLANES, pltpu.roll(y, SUBLANES - sh, 0), 1.0)
        off = jnp.where(sub + 1 < SUBLANES, pltpu.roll(y, SUBLANES - 1, 0), 1.0) * carry[:, lo:hi]
        off2 = jnp.concatenate([off, off], axis=0)
        for i in range(SB_K_TILE // (2 * SUBLANES)):
            rows = slice(i * 2 * SUBLANES, (i + 1) * 2 * SUBLANES)
            w_scr[slot, rows, lo:hi] = (p_scr[rows, lo:hi] * off2).astype(BF16)
        new = carry[:, lo:hi] * jnp.broadcast_to(y[0:1, :], (SUBLANES, width))
        parts = ([carry[:, :lo]] if lo > 0 else []) + [new] + ([carry[:, hi:]] if hi < tq else [])
        return new if len(parts) == 1 else jnp.concatenate(parts, axis=1)

    def values(j, slot, lo=0, hi=tq):
        o_scr[:, lo:hi] += jnp.dot(v_ref[0, j], w_scr[slot, :, lo:hi], preferred_element_type=F32)

    def first_lane(m):
        return (n_diag - 1 - m) * SB_K_TILE

    top = n_tiles - 1
    carry = jnp.ones((SUBLANES, tq), F32)
    logits(top, 0, first_lane(0))
    for p in range(n_diag // 2):
        m = 2 * p
        logits(top - m - 1, 1, first_lane(m + 1))
        carry = weights(top - m, 0, carry, True, first_lane(m))
        if p > 0:
            values(top - m + 1, 1, first_lane(m - 1))
        logits(jnp.maximum(top - m - 2, 0), 0, first_lane(m + 2) if m + 2 < n_diag else 0)
        carry = weights(top - m - 1, 1, carry, True, first_lane(m + 1))
        values(top - m, 0, first_lane(m))
    values(top - n_diag + 1, 1)

    def pair(state, lo, hi):
        p, carry, _ = state
        j = top - 2 * p
        logits(j - 1, 1, lo, hi)
        carry = weights(j, 0, carry, False, lo, hi)
        alive = jnp.max(carry[0:1, lo:hi]) > 0.0
        values(j, 0, lo, hi)
        logits(jnp.maximum(j - 2, 0), 0, lo, hi)
        carry = weights(j - 1, 1, carry, False, lo, hi)
        values(j - 1, 1, lo, hi)
        return p + 1, carry, alive

    n_pairs = n_tiles // 2
    narrow = jnp.max(carry[0:1, SB_K_TILE:]) == 0.0
    state = (jnp.int32(n_diag // 2), carry, jnp.bool_(True))
    state = lax.while_loop(lambda st: (st[0] < n_pairs) & st[2] & narrow,
                           functools.partial(pair, lo=0, hi=SB_K_TILE), state)
    lax.while_loop(lambda st: (st[0] < n_pairs) & st[2], functools.partial(pair, lo=0, hi=tq), state)
    o_ref[0] = o_scr[...]


def _sb_attention(q_t, k_p, v_t):
    b, _, s = q_t.shape
    dh = SB_HEAD_DIM
    nkt = s // SB_K_TILE
    tq = SB_Q_TILE
    r = np.arange(SB_K_TILE)
    key_in_tile = (r % SUBLANES) * SB_STRANDS + r // SUBLANES
    dmat = jnp.asarray(key_in_tile[:, None] - np.arange(tq)[None, :], dtype=jnp.int32)
    return pl.pallas_call(
        _sb_body,
        grid=(b, SB_HEADS, s // tq),
        in_specs=[
            pl.BlockSpec((1, dh, tq), lambda bi, hi, qi: (bi, hi, qi)),
            pl.BlockSpec((1, 1, nkt, SB_K_TILE, dh), lambda bi, hi, qi: (bi, hi, 0, 0, 0)),
            pl.BlockSpec((1, nkt, dh, SB_K_TILE), lambda bi, hi, qi: (bi, 0, hi, 0)),
            _resident((SB_K_TILE, tq)),
        ],
        out_specs=pl.BlockSpec((1, dh, tq), lambda bi, hi, qi: (bi, hi, qi)),
        out_shape=jax.ShapeDtypeStruct((b, SB_W, s), F32),
        scratch_shapes=[pltpu.VMEM((2, SB_K_TILE, tq), F32), pltpu.VMEM((SB_K_TILE, tq), F32), pltpu.VMEM((2, SB_K_TILE, tq), BF16),
                        pltpu.VMEM((dh, tq), F32)],
        compiler_params=pltpu.CompilerParams(dimension_semantics=("parallel", "parallel", "arbitrary"),
                                             vmem_limit_bytes=VMEM_LIMIT),
        name="stick_breaking",
    )(q_t, k_p, v_t, dmat)


def _ret_log_gamma(h):
    return math.log(1.0 - 2.0 ** (-5.0 - h))


def _ret_body(q_ref, k_ref, v_ref, g_ref, cc_ref, sc_ref, cr_ref, sr_ref, dec_ref, qd_ref, kd_ref, o_ref, st_scr):
    c = RET_CHUNK
    step = pl.program_id(0)

    @pl.when(step == 0)
    def _():
        st_scr[...] = jnp.zeros_like(st_scr)

    lane = lax.broadcasted_iota(jnp.int32, (c, RET_QK_W), 1)
    upper = (lane % RET_QK_DIM) >= RET_QK_DIM // 2

    cos_sin = []
    for ci in range(RET_STEP_CHUNKS):
        chunk = step * RET_STEP_CHUNKS + ci
        ca, sa = cc_ref[pl.ds(chunk, 1), :], sc_ref[pl.ds(chunk, 1), :]
        cos_sin.append((ca * cr_ref[...] - sa * sr_ref[...], sa * cr_ref[...] + ca * sr_ref[...]))

    for bi in range(q_ref.shape[0]):
        st = st_scr[bi]
        for ci in range(RET_STEP_CHUNKS):
            rs = slice(ci * c, (ci + 1) * c)
            cos, sin = cos_sin[ci]

            def rot(x):
                partner = jnp.where(upper, pltpu.roll(x, RET_QK_DIM // 2, 1),
                                    -pltpu.roll(x, RET_QK_W - RET_QK_DIM // 2, 1))
                return x * cos + partner * sin

            q = rot(q_ref[bi, rs, :])
            k = rot(k_ref[bi, rs, :])
            kb = k.astype(BF16)
            qdb = (q * qd_ref[...]).astype(BF16)
            kdt = (k * kd_ref[...]).T.astype(BF16)
            v = v_ref[bi, rs, :]
            stb = st.astype(BF16)
            kv_all = jnp.dot(kdt, v, preferred_element_type=F32)
            new_st = []
            for h in range(RET_HEADS):
                head = (lane >= h * RET_QK_DIM) & (lane < (h + 1) * RET_QK_DIM)
                qh = jnp.where(head, q, 0.0).astype(BF16)
                scores = lax.dot_general(qh, kb, (((1,), (1,)), ((), ())), preferred_element_type=F32)
                scores = (scores * dec_ref[h]).astype(BF16)
                vh = v[:, h * RET_V_DIM:(h + 1) * RET_V_DIM]
                intra = jnp.dot(scores, vh, preferred_element_type=F32)
                qdh = jnp.where(head, qdb, jnp.zeros_like(qdb))
                cross = jnp.dot(qdh, stb, preferred_element_type=F32)
                y = intra + cross
                y = y * lax.rsqrt(jnp.mean(y * y, axis=-1, keepdims=True) + EPS)
                gh = g_ref[bi, rs, h * RET_V_DIM:(h + 1) * RET_V_DIM]
                o_ref[bi, rs, h * RET_V_DIM:(h + 1) * RET_V_DIM] = (y * (gh * jax.nn.sigmoid(gh))).astype(BF16)
                rows = slice(h * RET_QK_DIM, (h + 1) * RET_QK_DIM)
                new_st.append(math.exp(c * _ret_log_gamma(h)) * st[rows, :]
                              + kv_all[rows, h * RET_V_DIM:(h + 1) * RET_V_DIM])
            st = jnp.concatenate(new_st, axis=0)
        st_scr[bi] = st


def _ret_tables(s):
    c = RET_CHUNK
    half = RET_QK_DIM // 2
    inv_freq = ROPE_BASE ** (-jnp.arange(half, dtype=F32) / half)

    def cos_sin(pos):
        ang = pos[:, None] * inv_freq[None, :]
        return (jnp.tile(jnp.cos(ang), (1, 2 * RET_HEADS)), jnp.tile(jnp.sin(ang), (1, 2 * RET_HEADS)))

    cc, sc = cos_sin(jnp.arange(s // c, dtype=F32) * c)
    cr, sr = cos_sin(jnp.arange(c, dtype=F32))

    idx = np.arange(c, dtype=np.float64)
    lg = np.array([_ret_log_gamma(h) for h in range(RET_HEADS)])
    diff = idx[:, None] - idx[None, :]
    scale = RET_QK_DIM ** -0.5
    dec = np.where(diff >= 0, np.exp(np.maximum(diff, 0.0)[None] * lg[:, None, None]), 0.0) * scale
    qd = np.repeat(np.exp((idx + 1)[:, None] * lg[None, :]), RET_QK_DIM, axis=1)
    kd = np.repeat(np.exp((c - 1 - idx)[:, None] * lg[None, :]), RET_QK_DIM, axis=1) * scale
    return cc, sc, cr, sr, jnp.asarray(dec, F32), jnp.asarray(qd, F32), jnp.asarray(kd, F32)


def _retention(q, k, v, g, tables, b, s):
    c = RET_CHUNK
    rows = c * RET_STEP_CHUNKS
    cc, sc, cr, sr, dec, qd, kd = tables

    def row(width):
        return pl.BlockSpec((b, rows, width), lambda ci: (0, ci, 0))

    def seq(a):
        return a.reshape(b, s, a.shape[-1])

    out = pl.pallas_call(
        _ret_body,
        grid=(s // rows,),
        in_specs=[row(RET_QK_W), row(RET_QK_W), row(RET_V_W), row(RET_V_W),
                  _resident((s // c, RET_QK_W)), _resident((s // c, RET_QK_W)),
                  _resident((c, RET_QK_W)), _resident((c, RET_QK_W)),
                  _resident((RET_HEADS, c, c)), _resident((c, RET_QK_W)), _resident((c, RET_QK_W))],
        out_specs=row(RET_V_W),
        out_shape=jax.ShapeDtypeStruct((b, s, RET_V_W), BF16),
        scratch_shapes=[pltpu.VMEM((b, RET_QK_W, RET_V_DIM), F32)],
        compiler_params=pltpu.CompilerParams(dimension_semantics=("arbitrary",), vmem_limit_bytes=VMEM_LIMIT),
        name="retention",
    )(seq(q), seq(k), seq(v), seq(g), cc, sc, cr, sr, dec, qd, kd)
    return out.reshape(b * s, RET_V_W)


def _merge_body(x_ref, ysb_ref, yret_ref, u_ref, halo_ref, gate_ref, wsb_ref, wret_ref, wpool_ref, pscale_ref,
                wbp_ref, wout_ref, o_ref, *, tiles_per_seq):
    tm = ROW_TILE
    i = pl.program_id(0)
    first = (i % tiles_per_seq) == 0
    u = u_ref[...]
    halo = jnp.where(first, 0.0, halo_ref[...])
    ext = jnp.concatenate([halo, u], axis=0)
    s2 = ext + pltpu.roll(ext, 1, 0)
    s4 = s2 + pltpu.roll(s2, 2, 0)
    s8 = s4 + pltpu.roll(s4, 4, 0)
    s16 = s8 + pltpu.roll(s8, 8, 0)
    lane = lax.broadcasted_iota(jnp.int32, (tm, POOL_W), 1)
    grp = lane // POOL_GROUP_DIM
    h0 = POOL_HALO
    win_sum = jnp.where(grp == 0, s2[h0:], jnp.where(grp == 1, s4[h0:], jnp.where(grp == 2, s8[h0:], s16[h0:])))
    window = jnp.where(grp == 0, POOL_WINDOWS[0],
                       jnp.where(grp == 1, POOL_WINDOWS[1], jnp.where(grp == 2, POOL_WINDOWS[2], POOL_WINDOWS[3])))
    t_pos = (i % tiles_per_seq) * tm + lax.broadcasted_iota(jnp.int32, (tm, POOL_W), 0)
    count = jnp.minimum(t_pos + 1, window).astype(F32)
    pooled = (win_sum / count - u).astype(BF16)
    y_pool = jnp.dot(pooled, wpool_ref[...], preferred_element_type=F32) * pscale_ref[...]

    def gated2(c, y):
        return y + jnp.tanh(gate_ref[:, c * D_MODEL:(c + 1) * D_MODEL].astype(F32)) * y

    y_sb = ysb_ref[0].T.astype(BF16)
    merged2 = gated2(0, jnp.dot(y_sb, wsb_ref[...], preferred_element_type=F32))
    merged2 += gated2(1, jnp.dot(yret_ref[...], wret_ref[...], preferred_element_type=F32))
    merged2 += gated2(2, jnp.dot(y_pool.astype(BF16), wbp_ref[...], preferred_element_type=F32))
    merged = (0.5 * merged2).astype(BF16)
    o_ref[...] = x_ref[...] + jnp.dot(merged, wout_ref[...], preferred_element_type=F32)


def _merge(x2, y_sb, y_ret, u, gate, wsb, wret, wpool_bd, pscale, wbp, wout, s, l):
    t = x2.shape[0]
    tm = ROW_TILE
    halo_per_tile = tm // POOL_HALO
    tps = s // tm

    def row(width):
        return pl.BlockSpec((tm, width), lambda i: (i, 0))

    halo = pl.BlockSpec((POOL_HALO, POOL_W), lambda i: (jnp.maximum(i * halo_per_tile - 1, 0), 0))
    return pl.pallas_call(
        functools.partial(_merge_body, tiles_per_seq=s // tm),
        grid=(t // tm,),
        in_specs=[row(D_MODEL), pl.BlockSpec((1, SB_W, tm), lambda i: (i // tps, 0, i % tps)),
                  row(RET_V_W), row(POOL_W), halo, row(N_BRANCHES * D_MODEL),
                  _layer(l, (SB_W, D_MODEL)), _layer(l, (RET_V_W, D_MODEL)), _resident((POOL_W, POOL_W)),
                  _resident((1, POOL_W)), _layer(l, (POOL_W, D_MODEL)), _layer(l, (D_MODEL, D_MODEL))],
        out_specs=row(D_MODEL),
        out_shape=jax.ShapeDtypeStruct((t, D_MODEL), F32),
        compiler_params=pltpu.CompilerParams(dimension_semantics=("parallel",), vmem_limit_bytes=VMEM_LIMIT),
        name="merge_out",
    )(x2, y_sb, y_ret, u, u, gate, wsb, wret, wpool_bd, pscale, wbp, wout)


def _block_diag(w_pool):
    out = jnp.zeros((POOL_W, POOL_W), w_pool.dtype)
    for g in range(POOL_GROUPS):
        sl = slice(g * POOL_GROUP_DIM, (g + 1) * POOL_GROUP_DIM)
        out = out.at[sl, sl].set(w_pool[g])
    return out


def kernel(x, g_ffn1, w1_ffn1, w3_ffn1, w2_ffn1, g_mix, w_in, w_branch_sb, w_branch_ret, w_branch_pool, w_pool,
           pool_scale, w_out, g_ffn2, w1_ffn2, w3_ffn2, w2_ffn2, g_final):
    b, s, d = x.shape
    assert d == D_MODEL and s % SB_Q_TILE == 0 and s % (RET_CHUNK * RET_STEP_CHUNKS) == 0 and s % ROW_TILE == 0
    depth = g_ffn1.shape[0]
    t = b * s
    assert t % FFN_ROW_TILE == 0
    x2 = x.reshape(t, d)
    tables = _ret_tables(s)
    gf = g_final.reshape(1, d)

    def bf(w):
        return w.astype(BF16)

    w1a, w3a, w2a = bf(w1_ffn1), bf(w3_ffn1), bf(w2_ffn1)
    w1b, w3b, w2b = bf(w1_ffn2), bf(w3_ffn2), bf(w2_ffn2)
    w_in_b, w_out_b = bf(w_in), bf(w_out)
    w_sb, w_ret, w_bp = bf(w_branch_sb), bf(w_branch_ret), bf(w_branch_pool)

    for l in range(depth):
        x2 = _ffn(x2, g_ffn1[l].reshape(1, d), w1a, w3a, w2a, gf, False, l)

        q_t, k_p, v_t, q_r, k_r, v_r, g_r, u_p, gate = _proj(x2, g_mix[l].reshape(1, d), w_in_b, b, s, l)
        y_sb = _sb_attention(q_t, k_p, v_t)

        y_ret = _retention(q_r, k_r, v_r, g_r, tables, b, s)

        x2 = _merge(x2, y_sb, y_ret, u_p, gate, w_sb, w_ret, bf(_block_diag(w_pool[l])),
                    pool_scale[l].reshape(1, POOL_W), w_bp, w_out_b, s, l)

        x2 = _ffn(x2, g_ffn2[l].reshape(1, d), w1b, w3b, w2b, gf, l == depth - 1, l)
    return x2.reshape(b, s, d)
```

```python
import functools
import math

import numpy as np
import jax
import jax.numpy as jnp
from jax import lax
from jax.experimental import pallas as pl
from jax.experimental.pallas import tpu as pltpu

F32 = jnp.float32
BF16 = jnp.bfloat16

D_MODEL = 1024
SB_HEADS = 4
SB_HEAD_DIM = 64
RET_HEADS = 4
RET_QK_DIM = 64
RET_V_DIM = 128
ROPE_BASE = 10000.0
POOL_GROUPS = 4
POOL_GROUP_DIM = 64
POOL_WINDOWS = (2, 4, 8, 16)
N_BRANCHES = 3
D_FF = 2816
EPS = 1e-6

SB_W = SB_HEADS * SB_HEAD_DIM
RET_QK_W = RET_HEADS * RET_QK_DIM
RET_V_W = RET_HEADS * RET_V_DIM
POOL_W = POOL_GROUPS * POOL_GROUP_DIM
D_IN = SB_W * 3 + RET_QK_W * 2 + RET_V_W * 2 + POOL_W + N_BRANCHES * D_MODEL

OFF_QSB = 0
OFF_KSB = OFF_QSB + SB_W
OFF_VSB = OFF_KSB + SB_W
OFF_QR = OFF_VSB + SB_W
OFF_KR = OFF_QR + RET_QK_W
OFF_VR = OFF_KR + RET_QK_W
OFF_GR = OFF_VR + RET_V_W
OFF_UP = OFF_GR + RET_V_W
OFF_GATE = OFF_UP + POOL_W

SUBLANES = 8
LANES = 128
MXU_DIM = 256
VMEM_LIMIT = 56 * 1024 * 1024

ROW_TILE = 1024
FFN_ROW_TILE = 1024
FF_CHUNK = 256
SB_Q_TILE = 1024
SB_K_TILE = 256
SB_HEADS_PER_STEP = 2
SB_STRANDS = SB_K_TILE // SUBLANES
RET_CHUNK = 256
RET_STEP_CHUNKS = 2
POOL_HALO = 16


def _resident(shape):
    nd = len(shape)
    return pl.BlockSpec(shape, lambda *_: (0,) * nd, pipeline_mode=pl.Buffered(1))


def _layer(l, shape):
    nd = len(shape)
    return pl.BlockSpec((None,) + tuple(shape), lambda *_: (l,) + (0,) * nd, pipeline_mode=pl.Buffered(1))


def _rms(x, g):
    ms = jnp.mean(x * x, axis=-1, keepdims=True)
    return x * lax.rsqrt(ms + EPS) * g


def _ffn_body(x_ref, g_ref, w1_ref, w3_ref, w2_ref, gf_ref, o_ref, acc_ref, *, final_norm):
    x = x_ref[...]
    h = _rms(x, g_ref[...]).astype(BF16)
    for c in range(D_FF // FF_CHUNK):
        sl = slice(c * FF_CHUNK, (c + 1) * FF_CHUNK)
        a = jnp.dot(h, w1_ref[:, sl], preferred_element_type=F32)
        b = jnp.dot(h, w3_ref[:, sl], preferred_element_type=F32)
        act = (a * jax.nn.sigmoid(a) * b).astype(BF16)
        part = jnp.dot(act, w2_ref[sl, :], preferred_element_type=F32)
        if c == 0:
            acc_ref[...] = part
        else:
            acc_ref[...] += part
    y = x + 0.5 * acc_ref[...]
    if final_norm:
        y = _rms(y, gf_ref[...])
    o_ref[...] = y


def _ffn(x2, g, w1, w3, w2, g_final, final_norm, l):
    t = x2.shape[0]
    tm = FFN_ROW_TILE
    row = pl.BlockSpec((tm, D_MODEL), lambda i: (i, 0))
    return pl.pallas_call(
        functools.partial(_ffn_body, final_norm=final_norm),
        grid=(t // tm,),
        in_specs=[row, _resident((1, D_MODEL)), _layer(l, (D_MODEL, D_FF)), _layer(l, (D_MODEL, D_FF)),
                  _layer(l, (D_FF, D_MODEL)), _resident((1, D_MODEL))],
        out_specs=row,
        out_shape=jax.ShapeDtypeStruct((t, D_MODEL), F32),
        scratch_shapes=[pltpu.VMEM((tm, D_MODEL), F32)],
        compiler_params=pltpu.CompilerParams(dimension_semantics=("parallel",), vmem_limit_bytes=VMEM_LIMIT),
        name="ffn_half",
    )(x2, g, w1, w3, w2, g_final)


def _proj_body(x_ref, g_ref, w_ref, qsb_ref, ksb_ref, vsb_ref, qr_ref, kr_ref, vr_ref, gr_ref, up_ref, gate_ref,
               perm_scr):
    h = _rms(x_ref[...], g_ref[...]).astype(BF16)

    def mm(lo, width):
        return jnp.dot(h, w_ref[:, lo:lo + width], preferred_element_type=F32)

    def stage(vals):
        for c in range(SB_W // LANES):
            perm_scr[c] = vals[:, c * LANES:(c + 1) * LANES]

    def strand_major(kt):
        base = kt * SB_K_TILE
        return jnp.concatenate(
            [jnp.concatenate([perm_scr[c, pl.ds(base + i, SUBLANES, stride=SB_STRANDS), :]
                              for i in range(SB_STRANDS)], axis=0) for c in range(SB_W // LANES)], axis=1)

    qsb_ref[0] = (mm(OFF_QSB, SB_W) * (SB_HEAD_DIM ** -0.5 * 0.5)).T.astype(BF16)
    stage(mm(OFF_KSB, SB_W))
    for kt in range(ROW_TILE // SB_K_TILE):
        k_tile = strand_major(kt).astype(BF16)
        for hd in range(SB_HEADS):
            ksb_ref[0, hd, kt] = k_tile[:, hd * SB_HEAD_DIM:(hd + 1) * SB_HEAD_DIM]
    stage(mm(OFF_VSB, SB_W))
    for kt in range(ROW_TILE // SB_K_TILE):
        vsb_ref[0, kt] = strand_major(kt).T.astype(BF16)
    qr_ref[...] = mm(OFF_QR, RET_QK_W)
    kr_ref[...] = mm(OFF_KR, RET_QK_W)
    vr_ref[...] = mm(OFF_VR, RET_V_W).astype(BF16)
    gr_ref[...] = mm(OFF_GR, RET_V_W)
    up_ref[...] = mm(OFF_UP, POOL_W)
    for c in range(N_BRANCHES):
        gate_ref[:, c * D_MODEL:(c + 1) * D_MODEL] = (0.5 * mm(OFF_GATE + c * D_MODEL, D_MODEL)).astype(BF16)


def _proj(x2, g, w_in, b, s, l):
    t = x2.shape[0]
    tm = ROW_TILE
    tps = s // tm
    kt_per_tile = tm // SB_K_TILE
    nkt = s // SB_K_TILE

    def row(width):
        return pl.BlockSpec((tm, width), lambda i: (i, 0))

    key_tiles = pl.BlockSpec((1, SB_HEADS, kt_per_tile, SB_K_TILE, SB_HEAD_DIM),
                             lambda i: (i // tps, 0, i % tps, 0, 0))
    val_tiles = pl.BlockSpec((1, kt_per_tile, SB_W, SB_K_TILE), lambda i: (i // tps, i % tps, 0, 0))
    widths = (RET_QK_W, RET_QK_W, RET_V_W, RET_V_W, POOL_W, N_BRANCHES * D_MODEL)
    dtypes = (F32, F32, BF16, F32, F32, BF16)
    return pl.pallas_call(
        _proj_body,
        grid=(t // tm,),
        in_specs=[row(D_MODEL), _resident((1, D_MODEL)), _layer(l, (D_MODEL, D_IN))],
        out_specs=[pl.BlockSpec((1, SB_W, tm), lambda i: (i // tps, 0, i % tps)), key_tiles, val_tiles]
                  + [row(w) for w in widths],
        out_shape=[jax.ShapeDtypeStruct((b, SB_W, s), BF16),
                   jax.ShapeDtypeStruct((b, SB_HEADS, nkt, SB_K_TILE, SB_HEAD_DIM), BF16),
                   jax.ShapeDtypeStruct((b, nkt, SB_W, SB_K_TILE), BF16)]
                  + [jax.ShapeDtypeStruct((t, w), d) for w, d in zip(widths, dtypes)],
        scratch_shapes=[pltpu.VMEM((SB_W // LANES, tm, LANES), F32)],
        compiler_params=pltpu.CompilerParams(dimension_semantics=("parallel",), vmem_limit_bytes=VMEM_LIMIT),
        name="mix_proj",
    )(x2, g, w_in)


def _sb_body(q_ref, k_ref, v_ref, d_ref, o_ref, z_scr, p_scr, w_scr, o_scr):
    qi = pl.program_id(2)
    tq = SB_Q_TILE
    n_diag = tq // SB_K_TILE
    n_tiles = (qi + 1) * n_diag
    o_scr[...] = jnp.zeros_like(o_scr)

    heads = range(SB_HEADS_PER_STEP)

    def head_rows(g):
        return slice(g * SB_HEAD_DIM, (g + 1) * SB_HEAD_DIM)

    def logits(g, j, slot, lo=0, hi=tq):
        z_scr[g, slot, :, lo:hi] = jnp.dot(k_ref[0, g, j], q_ref[0, head_rows(g), lo:hi],
                                           preferred_element_type=F32)

    def weights(g, j, slot, carry, masked, lo=0, hi=tq):
        width = hi - lo
        sub = lax.broadcasted_iota(jnp.int32, (SUBLANES, width), 0)
        thr = qi * tq - j * SB_K_TILE
        acc = jnp.ones((SUBLANES, width), F32)
        for i in range(SB_STRANDS - 1, -1, -1):
            rows = slice(i * SUBLANES, (i + 1) * SUBLANES)
            th = jnp.tanh(z_scr[g, slot, rows, lo:hi])
            if masked:
                th = jnp.where(d_ref[rows, lo:hi] < thr, th, -1.0)
            half = 0.5 * th
            p_scr[g, rows, lo:hi] = (0.5 + half) * acc
            acc = acc * (0.5 - half)
        y = acc
        for sh in (1, 2, 4):
            y = y * jnp.where(sub + sh < SUBLANES, pltpu.roll(y, SUBLANES - sh, 0), 1.0)
        off = jnp.where(sub + 1 < SUBLANES, pltpu.roll(y, SUBLANES - 1, 0), 1.0) * carry[:, lo:hi]
        off2 = jnp.concatenate([off, off], axis=0)
        for i in range(SB_K_TILE // (2 * SUBLANES)):
            rows = slice(i * 2 * SUBLANES, (i + 1) * 2 * SUBLANES)
            w_scr[g, slot, rows, lo:hi] = (p_scr[g, rows, lo:hi] * off2).astype(BF16)
        new = carry[:, lo:hi] * jnp.broadcast_to(y[0:1, :], (SUBLANES, width))
        parts = ([carry[:, :lo]] if lo > 0 else []) + [new] + ([carry[:, hi:]] if hi < tq else [])
        return new if len(parts) == 1 else jnp.concatenate(parts, axis=1)

    def values(g, j, slot, lo=0, hi=tq):
        o_scr[head_rows(g), lo:hi] += jnp.dot(v_ref[0, j, head_rows(g), :], w_scr[g, slot, :, lo:hi],
                                              preferred_element_type=F32)

    def first_lane(m):
        return (n_diag - 1 - m) * SB_K_TILE

    def near_lanes(m):
        return first_lane(m), min(first_lane(m) + 2 * SB_K_TILE, tq)

    def diag_weights(g, m, slot, carry):
        lo, hi = near_lanes(m)
        carry = weights(g, top - m, slot, carry, True, lo, lo + SB_K_TILE)
        if lo + SB_K_TILE < hi:
            carry = weights(g, top - m, slot, carry, False, lo + SB_K_TILE, hi)
        return carry

    def live(carry, lo, hi):
        return functools.reduce(jnp.maximum, [jnp.max(c[0:1, lo:hi]) for c in carry]) > 0.0

    top = n_tiles - 1
    carry = [jnp.ones((SUBLANES, tq), F32) for _ in heads]
    for g in heads:
        logits(g, top, 0, *near_lanes(0))
    for p in range(n_diag // 2):
        m = 2 * p
        for g in heads:
            logits(g, top - m - 1, 1, *near_lanes(m + 1))
            carry[g] = diag_weights(g, m, 0, carry[g])
            if p > 0:
                values(g, top - m + 1, 1, *near_lanes(m - 1))
        for g in heads:
            logits(g, jnp.maximum(top - m - 2, 0), 0, *(near_lanes(m + 2) if m + 2 < n_diag else (0, tq)))
            carry[g] = diag_weights(g, m + 1, 1, carry[g])
            values(g, top - m, 0, *near_lanes(m))
    for g in heads:
        values(g, top - n_diag + 1, 1, *near_lanes(n_diag - 1))

    if n_diag > 2:
        def far_tiles(carry):
            carry = list(carry)
            for m in range(2, n_diag):
                lo = first_lane(m) + 2 * SB_K_TILE
                for g in heads:
                    logits(g, top - m, 1, lo, tq)
                    carry[g] = weights(g, top - m, 1, carry[g], False, lo, tq)
                    values(g, top - m, 1, lo, tq)
            return tuple(carry)

        carry = list(lax.cond(live(carry, 2 * SB_K_TILE, tq), far_tiles, lambda c: c, tuple(carry)))


    def pair(state, lo, hi):
        p, carry, _ = state
        carry = list(carry)
        j = top - 2 * p
        for g in heads:
            logits(g, j - 1, 1, lo, hi)
            carry[g] = weights(g, j, 0, carry[g], False, lo, hi)
        alive = live(carry, lo, hi)
        for g in heads:
            values(g, j, 0, lo, hi)
            logits(g, jnp.maximum(j - 2, 0), 0, lo, hi)
            carry[g] = weights(g, j - 1, 1, carry[g], False, lo, hi)
            values(g, j - 1, 1, lo, hi)
        return p + 1, tuple(carry), alive

    n_pairs = n_tiles // 2
    narrow = jnp.logical_not(live(carry, SB_K_TILE, tq))
    state = (jnp.int32(n_diag // 2), tuple(carry), jnp.bool_(True))
    state = lax.while_loop(lambda st: (st[0] < n_pairs) & st[2] & narrow,
                           functools.partial(pair, lo=0, hi=SB_K_TILE), state)
    lax.while_loop(lambda st: (st[0] < n_pairs) & st[2], functools.partial(pair, lo=0, hi=tq), state)
    o_ref[0] = o_scr[...]


def _sb_attention(q_t, k_p, v_t):
    b, _, s = q_t.shape
    dh = SB_HEAD_DIM
    nkt = s // SB_K_TILE
    tq = SB_Q_TILE
    r = np.arange(SB_K_TILE)
    key_in_tile = (r % SUBLANES) * SB_STRANDS + r // SUBLANES
    dmat = jnp.asarray(key_in_tile[:, None] - np.arange(tq)[None, :], dtype=jnp.int32)
    hps = SB_HEADS_PER_STEP
    return pl.pallas_call(
        _sb_body,
        grid=(b, SB_HEADS // hps, s // tq),
        in_specs=[
            pl.BlockSpec((1, hps * dh, tq), lambda bi, hi, qi: (bi, hi, qi)),
            pl.BlockSpec((1, hps, nkt, SB_K_TILE, dh), lambda bi, hi, qi: (bi, hi, 0, 0, 0)),
            pl.BlockSpec((1, nkt, hps * dh, SB_K_TILE), lambda bi, hi, qi: (bi, 0, hi, 0)),
            _resident((SB_K_TILE, tq)),
        ],
        out_specs=pl.BlockSpec((1, hps * dh, tq), lambda bi, hi, qi: (bi, hi, qi)),
        out_shape=jax.ShapeDtypeStruct((b, SB_W, s), F32),
        scratch_shapes=[pltpu.VMEM((hps, 2, SB_K_TILE, tq), F32), pltpu.VMEM((hps, SB_K_TILE, tq), F32),
                        pltpu.VMEM((hps, 2, SB_K_TILE, tq), BF16), pltpu.VMEM((hps * dh, tq), F32)],
        compiler_params=pltpu.CompilerParams(dimension_semantics=("parallel", "parallel", "arbitrary"),
                                             vmem_limit_bytes=VMEM_LIMIT),
        name="stick_breaking",
    )(q_t, k_p, v_t, dmat)


def _ret_log_gamma(h):
    return math.log(1.0 - 2.0 ** (-5.0 - h))


def _ret_body(q_ref, k_ref, v_ref, g_ref, cc_ref, sc_ref, cr_ref, sr_ref, dec_ref, qd_ref, kd_ref, o_ref, st_scr):
    c = RET_CHUNK
    step = pl.program_id(0)

    @pl.when(step == 0)
    def _():
        st_scr[...] = jnp.zeros_like(st_scr)

    lane = lax.broadcasted_iota(jnp.int32, (c, RET_QK_W), 1)
    upper = (lane % RET_QK_DIM) >= RET_QK_DIM // 2

    cos_sin = []
    for ci in range(RET_STEP_CHUNKS):
        chunk = step * RET_STEP_CHUNKS + ci
        ca, sa = cc_ref[pl.ds(chunk, 1), :], sc_ref[pl.ds(chunk, 1), :]
        cos_sin.append((ca * cr_ref[...] - sa * sr_ref[...], sa * cr_ref[...] + ca * sr_ref[...]))

    for bi in range(q_ref.shape[0]):
        st = st_scr[bi]
        for ci in range(RET_STEP_CHUNKS):
            rs = slice(ci * c, (ci + 1) * c)
            cos, sin = cos_sin[ci]

            def rot(x):
                partner = jnp.where(upper, pltpu.roll(x, RET_QK_DIM // 2, 1),
                                    -pltpu.roll(x, RET_QK_W - RET_QK_DIM // 2, 1))
                return x * cos + partner * sin

            q = rot(q_ref[bi, rs, :])
            k = rot(k_ref[bi, rs, :])
            kb = k.astype(BF16)
            qdb = (q * qd_ref[...]).astype(BF16)
            kdt = (k * kd_ref[...]).T.astype(BF16)
            v = v_ref[bi, rs, :]
            stb = st.astype(BF16)
            kv_all = jnp.dot(kdt, v, preferred_element_type=F32)
            new_st = []
            for h in range(RET_HEADS):
                head = (lane >= h * RET_QK_DIM) & (lane < (h + 1) * RET_QK_DIM)
                qh = jnp.where(head, q, 0.0).astype(BF16)
                scores = lax.dot_general(qh, kb, (((1,), (1,)), ((), ())), preferred_element_type=F32)
                scores = (scores * dec_ref[h]).astype(BF16)
                vh = v[:, h * RET_V_DIM:(h + 1) * RET_V_DIM]
                intra = jnp.dot(scores, vh, preferred_element_type=F32)
                qdh = jnp.where(head, qdb, jnp.zeros_like(qdb))
                cross = jnp.dot(qdh, stb, preferred_element_type=F32)
                y = intra + cross
                y = y * lax.rsqrt(jnp.mean(y * y, axis=-1, keepdims=True) + EPS)
                gh = g_ref[bi, rs, h * RET_V_DIM:(h + 1) * RET_V_DIM]
                o_ref[bi, rs, h * RET_V_DIM:(h + 1) * RET_V_DIM] = (y * (gh * jax.nn.sigmoid(gh))).astype(BF16)
                rows = slice(h * RET_QK_DIM, (h + 1) * RET_QK_DIM)
                new_st.append(math.exp(c * _ret_log_gamma(h)) * st[rows, :]
                              + kv_all[rows, h * RET_V_DIM:(h + 1) * RET_V_DIM])
            st = jnp.concatenate(new_st, axis=0)
        st_scr[bi] = st


def _ret_tables(s):
    c = RET_CHUNK
    half = RET_QK_DIM // 2
    inv_freq = ROPE_BASE ** (-jnp.arange(half, dtype=F32) / half)

    def cos_sin(pos):
        ang = pos[:, None] * inv_freq[None, :]
        return (jnp.tile(jnp.cos(ang), (1, 2 * RET_HEADS)), jnp.tile(jnp.sin(ang), (1, 2 * RET_HEADS)))

    cc, sc = cos_sin(jnp.arange(s // c, dtype=F32) * c)
    cr, sr = cos_sin(jnp.arange(c, dtype=F32))

    idx = np.arange(c, dtype=np.float64)
    lg = np.array([_ret_log_gamma(h) for h in range(RET_HEADS)])
    diff = idx[:, None] - idx[None, :]
    scale = RET_QK_DIM ** -0.5
    dec = np.where(diff >= 0, np.exp(np.maximum(diff, 0.0)[None] * lg[:, None, None]), 0.0) * scale
    qd = np.repeat(np.exp((idx + 1)[:, None] * lg[None, :]), RET_QK_DIM, axis=1)
    kd = np.repeat(np.exp((c - 1 - idx)[:, None] * lg[None, :]), RET_QK_DIM, axis=1) * scale
    return cc, sc, cr, sr, jnp.asarray(dec, F32), jnp.asarray(qd, F32), jnp.asarray(kd, F32)


def _retention(q, k, v, g, tables, b, s):
    c = RET_CHUNK
    rows = c * RET_STEP_CHUNKS
    cc, sc, cr, sr, dec, qd, kd = tables

    def row(width):
        return pl.BlockSpec((b, rows, width), lambda ci: (0, ci, 0))

    def seq(a):
        return a.reshape(b, s, a.shape[-1])

    out = pl.pallas_call(
        _ret_body,
        grid=(s // rows,),
        in_specs=[row(RET_QK_W), row(RET_QK_W), row(RET_V_W), row(RET_V_W),
                  _resident((s // c, RET_QK_W)), _resident((s // c, RET_QK_W)),
                  _resident((c, RET_QK_W)), _resident((c, RET_QK_W)),
                  _resident((RET_HEADS, c, c)), _resident((c, RET_QK_W)), _resident((c, RET_QK_W))],
        out_specs=row(RET_V_W),
        out_shape=jax.ShapeDtypeStruct((b, s, RET_V_W), BF16),
        scratch_shapes=[pltpu.VMEM((b, RET_QK_W, RET_V_DIM), F32)],
        compiler_params=pltpu.CompilerParams(dimension_semantics=("arbitrary",), vmem_limit_bytes=VMEM_LIMIT),
        name="retention",
    )(seq(q), seq(k), seq(v), seq(g), cc, sc, cr, sr, dec, qd, kd)
    return out.reshape(b * s, RET_V_W)


def _merge_body(x_ref, ysb_ref, yret_ref, u_ref, halo_ref, gate_ref, wsb_ref, wret_ref, wpool_ref, pscale_ref,
                wbp_ref, wout_ref, o_ref, *, tiles_per_seq):
    tm = ROW_TILE
    i = pl.program_id(0)
    first = (i % tiles_per_seq) == 0
    u = u_ref[...]
    halo = jnp.where(first, 0.0, halo_ref[...])
    ext = jnp.concatenate([halo, u], axis=0)
    s2 = ext + pltpu.roll(ext, 1, 0)
    s4 = s2 + pltpu.roll(s2, 2, 0)
    s8 = s4 + pltpu.roll(s4, 4, 0)
    s16 = s8 + pltpu.roll(s8, 8, 0)
    lane = lax.broadcasted_iota(jnp.int32, (tm, POOL_W), 1)
    grp = lane // POOL_GROUP_DIM
    h0 = POOL_HALO
    win_sum = jnp.where(grp == 0, s2[h0:], jnp.where(grp == 1, s4[h0:], jnp.where(grp == 2, s8[h0:], s16[h0:])))
    window = jnp.where(grp == 0, POOL_WINDOWS[0],
                       jnp.where(grp == 1, POOL_WINDOWS[1], jnp.where(grp == 2, POOL_WINDOWS[2], POOL_WINDOWS[3])))
    t_pos = (i % tiles_per_seq) * tm + lax.broadcasted_iota(jnp.int32, (tm, POOL_W), 0)
    count = jnp.minimum(t_pos + 1, window).astype(F32)
    pooled = (win_sum / count - u).astype(BF16)
    y_pool = jnp.dot(pooled, wpool_ref[...], preferred_element_type=F32) * pscale_ref[...]

    def gated2(c, y):
        return y + jnp.tanh(gate_ref[:, c * D_MODEL:(c + 1) * D_MODEL].astype(F32)) * y

    y_sb = ysb_ref[0].T.astype(BF16)
    merged2 = gated2(0, jnp.dot(y_sb, wsb_ref[...], preferred_element_type=F32))
    merged2 += gated2(1, jnp.dot(yret_ref[...], wret_ref[...], preferred_element_type=F32))
    merged2 += gated2(2, jnp.dot(y_pool.astype(BF16), wbp_ref[...], preferred_element_type=F32))
    merged = (0.5 * merged2).astype(BF16)
    o_ref[...] = x_ref[...] + jnp.dot(merged, wout_ref[...], preferred_element_type=F32)


def _merge(x2, y_sb, y_ret, u, gate, wsb, wret, wpool_bd, pscale, wbp, wout, s, l):
    t = x2.shape[0]
    tm = ROW_TILE
    halo_per_tile = tm // POOL_HALO
    tps = s // tm

    def row(width):
        return pl.BlockSpec((tm, width), lambda i: (i, 0))

    halo = pl.BlockSpec((POOL_HALO, POOL_W), lambda i: (jnp.maximum(i * halo_per_tile - 1, 0), 0))
    return pl.pallas_call(
        functools.partial(_merge_body, tiles_per_seq=s // tm),
        grid=(t // tm,),
        in_specs=[row(D_MODEL), pl.BlockSpec((1, SB_W, tm), lambda i: (i // tps, 0, i % tps)),
                  row(RET_V_W), row(POOL_W), halo, row(N_BRANCHES * D_MODEL),
                  _layer(l, (SB_W, D_MODEL)), _layer(l, (RET_V_W, D_MODEL)), _resident((POOL_W, POOL_W)),
                  _resident((1, POOL_W)), _layer(l, (POOL_W, D_MODEL)), _layer(l, (D_MODEL, D_MODEL))],
        out_specs=row(D_MODEL),
        out_shape=jax.ShapeDtypeStruct((t, D_MODEL), F32),
        compiler_params=pltpu.CompilerParams(dimension_semantics=("parallel",), vmem_limit_bytes=VMEM_LIMIT),
        name="merge_out",
    )(x2, y_sb, y_ret, u, u, gate, wsb, wret, wpool_bd, pscale, wbp, wout)


def _block_diag(w_pool):
    out = jnp.zeros((POOL_W, POOL_W), w_pool.dtype)
    for g in range(POOL_GROUPS):
        sl = slice(g * POOL_GROUP_DIM, (g + 1) * POOL_GROUP_DIM)
        out = out.at[sl, sl].set(w_pool[g])
    return out


def kernel(x, g_ffn1, w1_ffn1, w3_ffn1, w2_ffn1, g_mix, w_in, w_branch_sb, w_branch_ret, w_branch_pool, w_pool,
           pool_scale, w_out, g_ffn2, w1_ffn2, w3_ffn2, w2_ffn2, g_final):
    b, s, d = x.shape
    assert d == D_MODEL and s % SB_Q_TILE == 0 and s % (RET_CHUNK * RET_STEP_CHUNKS) == 0 and s % ROW_TILE == 0
    depth = g_ffn1.shape[0]
    t = b * s
    assert t % FFN_ROW_TILE == 0
    x2 = x.reshape(t, d)
    tables = _ret_tables(s)
    gf = g_final.reshape(1, d)

    def bf(w):
        return w.astype(BF16)

    w1a, w3a, w2a = bf(w1_ffn1), bf(w3_ffn1), bf(w2_ffn1)
    w1b, w3b, w2b = bf(w1_ffn2), bf(w3_ffn2), bf(w2_ffn2)
    w_in_b, w_out_b = bf(w_in), bf(w_out)
    w_sb, w_ret, w_bp = bf(w_branch_sb), bf(w_branch_ret), bf(w_branch_pool)

    for l in range(depth):
        x2 = _ffn(x2, g_ffn1[l].reshape(1, d), w1a, w3a, w2a, gf, False, l)

        q_t, k_p, v_t, q_r, k_r, v_r, g_r, u_p, gate = _proj(x2, g_mix[l].reshape(1, d), w_in_b, b, s, l)
        y_sb = _sb_attention(q_t, k_p, v_t)

        y_ret = _retention(q_r, k_r, v_r, g_r, tables, b, s)

        x2 = _merge(x2, y_sb, y_ret, u_p, gate, w_sb, w_ret, bf(_block_diag(w_pool[l])),
                    pool_scale[l].reshape(1, POOL_W), w_bp, w_out_b, s, l)

        x2 = _ffn(x2, g_ffn2[l].reshape(1, d), w1b, w3b, w2b, gf, l == depth - 1, l)
    return x2.reshape(b, s, d)
```

```python
import functools
import math

import numpy as np
import jax
import jax.numpy as jnp
from jax import lax
from jax.experimental import pallas as pl
from jax.experimental.pallas import tpu as pltpu

F32 = jnp.float32
BF16 = jnp.bfloat16

D_MODEL = 1024
SB_HEADS = 4
SB_HEAD_DIM = 64
RET_HEADS = 4
RET_QK_DIM = 64
RET_V_DIM = 128
ROPE_BASE = 10000.0
POOL_GROUPS = 4
POOL_GROUP_DIM = 64
POOL_WINDOWS = (2, 4, 8, 16)
N_BRANCHES = 3
D_FF = 2816
EPS = 1e-6

SB_W = SB_HEADS * SB_HEAD_DIM
RET_QK_W = RET_HEADS * RET_QK_DIM
RET_V_W = RET_HEADS * RET_V_DIM
POOL_W = POOL_GROUPS * POOL_GROUP_DIM
D_IN = SB_W * 3 + RET_QK_W * 2 + RET_V_W * 2 + POOL_W + N_BRANCHES * D_MODEL

OFF_QSB = 0
OFF_KSB = OFF_QSB + SB_W
OFF_VSB = OFF_KSB + SB_W
OFF_QR = OFF_VSB + SB_W
OFF_KR = OFF_QR + RET_QK_W
OFF_VR = OFF_KR + RET_QK_W
OFF_GR = OFF_VR + RET_V_W
OFF_UP = OFF_GR + RET_V_W
OFF_GATE = OFF_UP + POOL_W

SUBLANES = 8
LANES = 128
MXU_DIM = 256
VMEM_LIMIT = 56 * 1024 * 1024

ROW_TILE = 1024
FFN_ROW_TILE = 1024
FF_CHUNK = 256
SB_Q_TILE = 2048
SB_K_TILE = 256
SB_HEADS_PER_STEP = 2
SB_STRANDS = SB_K_TILE // SUBLANES
RET_CHUNK = 256
RET_STEP_CHUNKS = 2
POOL_HALO = 16


def _resident(shape):
    nd = len(shape)
    return pl.BlockSpec(shape, lambda *_: (0,) * nd, pipeline_mode=pl.Buffered(1))


def _layer(l, shape):
    nd = len(shape)
    return pl.BlockSpec((None,) + tuple(shape), lambda *_: (l,) + (0,) * nd, pipeline_mode=pl.Buffered(1))


def _rms(x, g):
    ms = jnp.mean(x * x, axis=-1, keepdims=True)
    return x * lax.rsqrt(ms + EPS) * g


def _ffn_body(x_ref, g_ref, w1_ref, w3_ref, w2_ref, gf_ref, o_ref, acc_ref, *, final_norm):
    x = x_ref[...]
    h = _rms(x, g_ref[...]).astype(BF16)
    for c in range(D_FF // FF_CHUNK):
        sl = slice(c * FF_CHUNK, (c + 1) * FF_CHUNK)
        a = jnp.dot(h, w1_ref[:, sl], preferred_element_type=F32)
        b = jnp.dot(h, w3_ref[:, sl], preferred_element_type=F32)
        act = (a * jax.nn.sigmoid(a) * b).astype(BF16)
        part = jnp.dot(act, w2_ref[sl, :], preferred_element_type=F32)
        if c == 0:
            acc_ref[...] = part
        else:
            acc_ref[...] += part
    y = x + 0.5 * acc_ref[...]
    if final_norm:
        y = _rms(y, gf_ref[...])
    o_ref[...] = y


def _ffn(x2, g, w1, w3, w2, g_final, final_norm, l):
    t = x2.shape[0]
    tm = FFN_ROW_TILE
    row = pl.BlockSpec((tm, D_MODEL), lambda i: (i, 0))
    return pl.pallas_call(
        functools.partial(_ffn_body, final_norm=final_norm),
        grid=(t // tm,),
        in_specs=[row, _resident((1, D_MODEL)), _layer(l, (D_MODEL, D_FF)), _layer(l, (D_MODEL, D_FF)),
                  _layer(l, (D_FF, D_MODEL)), _resident((1, D_MODEL))],
        out_specs=row,
        out_shape=jax.ShapeDtypeStruct((t, D_MODEL), F32),
        scratch_shapes=[pltpu.VMEM((tm, D_MODEL), F32)],
        compiler_params=pltpu.CompilerParams(dimension_semantics=("parallel",), vmem_limit_bytes=VMEM_LIMIT),
        name="ffn_half",
    )(x2, g, w1, w3, w2, g_final)


def _proj_body(x_ref, g_ref, w_ref, qsb_ref, ksb_ref, vsb_ref, qr_ref, kr_ref, vr_ref, gr_ref, up_ref, gate_ref,
               perm_scr):
    h = _rms(x_ref[...], g_ref[...]).astype(BF16)

    def mm(lo, width):
        return jnp.dot(h, w_ref[:, lo:lo + width], preferred_element_type=F32)

    def stage(vals):
        for c in range(SB_W // LANES):
            perm_scr[c] = vals[:, c * LANES:(c + 1) * LANES]

    def strand_major(kt):
        base = kt * SB_K_TILE
        return jnp.concatenate(
            [jnp.concatenate([perm_scr[c, pl.ds(base + i, SUBLANES, stride=SB_STRANDS), :]
                              for i in range(SB_STRANDS)], axis=0) for c in range(SB_W // LANES)], axis=1)

    qsb_ref[0] = (mm(OFF_QSB, SB_W) * (SB_HEAD_DIM ** -0.5 * 0.5)).T.astype(BF16)
    stage(mm(OFF_KSB, SB_W))
    for kt in range(ROW_TILE // SB_K_TILE):
        k_tile = strand_major(kt).astype(BF16)
        for hd in range(SB_HEADS):
            ksb_ref[0, hd, kt] = k_tile[:, hd * SB_HEAD_DIM:(hd + 1) * SB_HEAD_DIM]
    stage(mm(OFF_VSB, SB_W))
    for kt in range(ROW_TILE // SB_K_TILE):
        vsb_ref[0, kt] = strand_major(kt).T.astype(BF16)
    qr_ref[...] = mm(OFF_QR, RET_QK_W)
    kr_ref[...] = mm(OFF_KR, RET_QK_W)
    vr_ref[...] = mm(OFF_VR, RET_V_W).astype(BF16)
    gr_ref[...] = mm(OFF_GR, RET_V_W)
    up_ref[...] = mm(OFF_UP, POOL_W)
    for c in range(N_BRANCHES):
        gate_ref[:, c * D_MODEL:(c + 1) * D_MODEL] = (0.5 * mm(OFF_GATE + c * D_MODEL, D_MODEL)).astype(BF16)


def _proj(x2, g, w_in, b, s, l):
    t = x2.shape[0]
    tm = ROW_TILE
    tps = s // tm
    kt_per_tile = tm // SB_K_TILE
    nkt = s // SB_K_TILE

    def row(width):
        return pl.BlockSpec((tm, width), lambda i: (i, 0))

    key_tiles = pl.BlockSpec((1, SB_HEADS, kt_per_tile, SB_K_TILE, SB_HEAD_DIM),
                             lambda i: (i // tps, 0, i % tps, 0, 0))
    val_tiles = pl.BlockSpec((1, kt_per_tile, SB_W, SB_K_TILE), lambda i: (i // tps, i % tps, 0, 0))
    widths = (RET_QK_W, RET_QK_W, RET_V_W, RET_V_W, POOL_W, N_BRANCHES * D_MODEL)
    dtypes = (F32, F32, BF16, F32, F32, BF16)
    return pl.pallas_call(
        _proj_body,
        grid=(t // tm,),
        in_specs=[row(D_MODEL), _resident((1, D_MODEL)), _layer(l, (D_MODEL, D_IN))],
        out_specs=[pl.BlockSpec((1, SB_W, tm), lambda i: (i // tps, 0, i % tps)), key_tiles, val_tiles]
                  + [row(w) for w in widths],
        out_shape=[jax.ShapeDtypeStruct((b, SB_W, s), BF16),
                   jax.ShapeDtypeStruct((b, SB_HEADS, nkt, SB_K_TILE, SB_HEAD_DIM), BF16),
                   jax.ShapeDtypeStruct((b, nkt, SB_W, SB_K_TILE), BF16)]
                  + [jax.ShapeDtypeStruct((t, w), d) for w, d in zip(widths, dtypes)],
        scratch_shapes=[pltpu.VMEM((SB_W // LANES, tm, LANES), F32)],
        compiler_params=pltpu.CompilerParams(dimension_semantics=("parallel",), vmem_limit_bytes=VMEM_LIMIT),
        name="mix_proj",
    )(x2, g, w_in)


def _sb_body(q_ref, k_ref, v_ref, d_ref, o_ref, z_scr, p_scr, w_scr, o_scr):
    qi = pl.program_id(2)
    tq = SB_Q_TILE
    n_diag = tq // SB_K_TILE
    n_tiles = (qi + 1) * n_diag
    o_scr[...] = jnp.zeros_like(o_scr)

    heads = range(SB_HEADS_PER_STEP)

    def head_rows(g):
        return slice(g * SB_HEAD_DIM, (g + 1) * SB_HEAD_DIM)

    def logits(g, j, slot, lo=0, hi=tq):
        z_scr[g, slot, :, lo:hi] = jnp.dot(k_ref[0, g, j], q_ref[0, head_rows(g), lo:hi],
                                           preferred_element_type=F32)

    def weights(g, j, slot, carry, masked, lo=0, hi=tq):
        width = hi - lo
        sub = lax.broadcasted_iota(jnp.int32, (SUBLANES, width), 0)
        thr = qi * tq - j * SB_K_TILE
        acc = jnp.ones((SUBLANES, width), F32)
        for i in range(SB_STRANDS - 1, -1, -1):
            rows = slice(i * SUBLANES, (i + 1) * SUBLANES)
            th = jnp.tanh(z_scr[g, slot, rows, lo:hi])
            if masked:
                th = jnp.where(d_ref[rows, lo:hi] < thr, th, -1.0)
            half = 0.5 * th
            p_scr[g, rows, lo:hi] = (0.5 + half) * acc
            acc = acc * (0.5 - half)
        y = acc
        for sh in (1, 2, 4):
            y = y * jnp.where(sub + sh < SUBLANES, pltpu.roll(y, SUBLANES - sh, 0), 1.0)
        off = jnp.where(sub + 1 < SUBLANES, pltpu.roll(y, SUBLANES - 1, 0), 1.0) * carry[:, lo:hi]
        off2 = jnp.concatenate([off, off], axis=0)
        for i in range(SB_K_TILE // (2 * SUBLANES)):
            rows = slice(i * 2 * SUBLANES, (i + 1) * 2 * SUBLANES)
            w_scr[g, slot, rows, lo:hi] = (p_scr[g, rows, lo:hi] * off2).astype(BF16)
        new = carry[:, lo:hi] * jnp.broadcast_to(y[0:1, :], (SUBLANES, width))
        parts = ([carry[:, :lo]] if lo > 0 else []) + [new] + ([carry[:, hi:]] if hi < tq else [])
        return new if len(parts) == 1 else jnp.concatenate(parts, axis=1)

    def values(g, j, slot, lo=0, hi=tq):
        o_scr[head_rows(g), lo:hi] += jnp.dot(v_ref[0, j, head_rows(g), :], w_scr[g, slot, :, lo:hi],
                                              preferred_element_type=F32)

    def first_lane(m):
        return (n_diag - 1 - m) * SB_K_TILE

    def near_lanes(m):
        return first_lane(m), min(first_lane(m) + 2 * SB_K_TILE, tq)

    def diag_weights(g, m, slot, carry):
        lo, hi = near_lanes(m)
        carry = weights(g, top - m, slot, carry, True, lo, lo + SB_K_TILE)
        if lo + SB_K_TILE < hi:
            carry = weights(g, top - m, slot, carry, False, lo + SB_K_TILE, hi)
        return carry

    def live(carry, lo, hi):
        return functools.reduce(jnp.maximum, [jnp.max(c[0:1, lo:hi]) for c in carry]) > 0.0

    top = n_tiles - 1
    carry = [jnp.ones((SUBLANES, tq), F32) for _ in heads]
    for g in heads:
        logits(g, top, 0, *near_lanes(0))
    for p in range(n_diag // 2):
        m = 2 * p
        for g in heads:
            logits(g, top - m - 1, 1, *near_lanes(m + 1))
            carry[g] = diag_weights(g, m, 0, carry[g])
            if p > 0:
                values(g, top - m + 1, 1, *near_lanes(m - 1))
        for g in heads:
            logits(g, jnp.maximum(top - m - 2, 0), 0, *(near_lanes(m + 2) if m + 2 < n_diag else (0, tq)))
            carry[g] = diag_weights(g, m + 1, 1, carry[g])
            values(g, top - m, 0, *near_lanes(m))
    for g in heads:
        values(g, top - n_diag + 1, 1, *near_lanes(n_diag - 1))

    if n_diag > 2:
        def far_tiles(carry):
            carry = list(carry)
            for m in range(2, n_diag):
                lo = first_lane(m) + 2 * SB_K_TILE
                for g in heads:
                    logits(g, top - m, 1, lo, tq)
                    carry[g] = weights(g, top - m, 1, carry[g], False, lo, tq)
                    values(g, top - m, 1, lo, tq)
            return tuple(carry)

        carry = list(lax.cond(live(carry, 2 * SB_K_TILE, tq), far_tiles, lambda c: c, tuple(carry)))


    def pair(state, lo, hi):
        p, carry, _ = state
        carry = list(carry)
        j = top - 2 * p
        for g in heads:
            logits(g, j - 1, 1, lo, hi)
            carry[g] = weights(g, j, 0, carry[g], False, lo, hi)
        alive = live(carry, lo, hi)
        for g in heads:
            values(g, j, 0, lo, hi)
            logits(g, jnp.maximum(j - 2, 0), 0, lo, hi)
            carry[g] = weights(g, j - 1, 1, carry[g], False, lo, hi)
            values(g, j - 1, 1, lo, hi)
        return p + 1, tuple(carry), alive

    n_pairs = n_tiles // 2
    narrow = jnp.logical_not(live(carry, SB_K_TILE, tq))
    state = (jnp.int32(n_diag // 2), tuple(carry), jnp.bool_(True))
    state = lax.while_loop(lambda st: (st[0] < n_pairs) & st[2] & narrow,
                           functools.partial(pair, lo=0, hi=SB_K_TILE), state)
    lax.while_loop(lambda st: (st[0] < n_pairs) & st[2], functools.partial(pair, lo=0, hi=tq), state)
    o_ref[0] = o_scr[...]


def _sb_attention(q_t, k_p, v_t):
    b, _, s = q_t.shape
    dh = SB_HEAD_DIM
    nkt = s // SB_K_TILE
    tq = SB_Q_TILE
    r = np.arange(SB_K_TILE)
    key_in_tile = (r % SUBLANES) * SB_STRANDS + r // SUBLANES
    dmat = jnp.asarray(key_in_tile[:, None] - np.arange(tq)[None, :], dtype=jnp.int32)
    hps = SB_HEADS_PER_STEP
    return pl.pallas_call(
        _sb_body,
        grid=(b, SB_HEADS // hps, s // tq),
        in_specs=[
            pl.BlockSpec((1, hps * dh, tq), lambda bi, hi, qi: (bi, hi, qi)),
            pl.BlockSpec((1, hps, nkt, SB_K_TILE, dh), lambda bi, hi, qi: (bi, hi, 0, 0, 0)),
            pl.BlockSpec((1, nkt, hps * dh, SB_K_TILE), lambda bi, hi, qi: (bi, 0, hi, 0)),
            _resident((SB_K_TILE, tq)),
        ],
        out_specs=pl.BlockSpec((1, hps * dh, tq), lambda bi, hi, qi: (bi, hi, qi)),
        out_shape=jax.ShapeDtypeStruct((b, SB_W, s), F32),
        scratch_shapes=[pltpu.VMEM((hps, 2, SB_K_TILE, tq), F32), pltpu.VMEM((hps, SB_K_TILE, tq), F32),
                        pltpu.VMEM((hps, 2, SB_K_TILE, tq), BF16), pltpu.VMEM((hps * dh, tq), F32)],
        compiler_params=pltpu.CompilerParams(dimension_semantics=("parallel", "parallel", "arbitrary"),
                                             vmem_limit_bytes=VMEM_LIMIT),
        name="stick_breaking",
    )(q_t, k_p, v_t, dmat)


def _ret_log_gamma(h):
    return math.log(1.0 - 2.0 ** (-5.0 - h))


def _ret_body(q_ref, k_ref, v_ref, g_ref, cc_ref, sc_ref, cr_ref, sr_ref, dec_ref, qd_ref, kd_ref, o_ref, st_scr):
    c = RET_CHUNK
    step = pl.program_id(0)

    @pl.when(step == 0)
    def _():
        st_scr[...] = jnp.zeros_like(st_scr)

    lane = lax.broadcasted_iota(jnp.int32, (c, RET_QK_W), 1)
    upper = (lane % RET_QK_DIM) >= RET_QK_DIM // 2

    cos_sin = []
    for ci in range(RET_STEP_CHUNKS):
        chunk = step * RET_STEP_CHUNKS + ci
        ca, sa = cc_ref[pl.ds(chunk, 1), :], sc_ref[pl.ds(chunk, 1), :]
        cos_sin.append((ca * cr_ref[...] - sa * sr_ref[...], sa * cr_ref[...] + ca * sr_ref[...]))

    for bi in range(q_ref.shape[0]):
        st = st_scr[bi]
        for ci in range(RET_STEP_CHUNKS):
            rs = slice(ci * c, (ci + 1) * c)
            cos, sin = cos_sin[ci]

            def rot(x):
                partner = jnp.where(upper, pltpu.roll(x, RET_QK_DIM // 2, 1),
                                    -pltpu.roll(x, RET_QK_W - RET_QK_DIM // 2, 1))
                return x * cos + partner * sin

            q = rot(q_ref[bi, rs, :])
            k = rot(k_ref[bi, rs, :])
            kb = k.astype(BF16)
            qdb = (q * qd_ref[...]).astype(BF16)
            kdt = (k * kd_ref[...]).T.astype(BF16)
            v = v_ref[bi, rs, :]
            stb = st.astype(BF16)
            kv_all = jnp.dot(kdt, v, preferred_element_type=F32)
            new_st = []
            for h in range(RET_HEADS):
                head = (lane >= h * RET_QK_DIM) & (lane < (h + 1) * RET_QK_DIM)
                qh = jnp.where(head, q, 0.0).astype(BF16)
                scores = lax.dot_general(qh, kb, (((1,), (1,)), ((), ())), preferred_element_type=F32)
                scores = (scores * dec_ref[h]).astype(BF16)
                vh = v[:, h * RET_V_DIM:(h + 1) * RET_V_DIM]
                intra = jnp.dot(scores, vh, preferred_element_type=F32)
                qdh = jnp.where(head, qdb, jnp.zeros_like(qdb))
                cross = jnp.dot(qdh, stb, preferred_element_type=F32)
                y = intra + cross
                y = y * lax.rsqrt(jnp.mean(y * y, axis=-1, keepdims=True) + EPS)
                gh = g_ref[bi, rs, h * RET_V_DIM:(h + 1) * RET_V_DIM]
                o_ref[bi, rs, h * RET_V_DIM:(h + 1) * RET_V_DIM] = (y * (gh * jax.nn.sigmoid(gh))).astype(BF16)
                rows = slice(h * RET_QK_DIM, (h + 1) * RET_QK_DIM)
                new_st.append(math.exp(c * _ret_log_gamma(h)) * st[rows, :]
                              + kv_all[rows, h * RET_V_DIM:(h + 1) * RET_V_DIM])
            st = jnp.concatenate(new_st, axis=0)
        st_scr[bi] = st


def _ret_tables(s):
    c = RET_CHUNK
    half = RET_QK_DIM // 2
    inv_freq = ROPE_BASE ** (-jnp.arange(half, dtype=F32) / half)

    def cos_sin(pos):
        ang = pos[:, None] * inv_freq[None, :]
        return (jnp.tile(jnp.cos(ang), (1, 2 * RET_HEADS)), jnp.tile(jnp.sin(ang), (1, 2 * RET_HEADS)))

    cc, sc = cos_sin(jnp.arange(s // c, dtype=F32) * c)
    cr, sr = cos_sin(jnp.arange(c, dtype=F32))

    idx = np.arange(c, dtype=np.float64)
    lg = np.array([_ret_log_gamma(h) for h in range(RET_HEADS)])
    diff = idx[:, None] - idx[None, :]
    scale = RET_QK_DIM ** -0.5
    dec = np.where(diff >= 0, np.exp(np.maximum(diff, 0.0)[None] * lg[:, None, None]), 0.0) * scale
    qd = np.repeat(np.exp((idx + 1)[:, None] * lg[None, :]), RET_QK_DIM, axis=1)
    kd = np.repeat(np.exp((c - 1 - idx)[:, None] * lg[None, :]), RET_QK_DIM, axis=1) * scale
    return cc, sc, cr, sr, jnp.asarray(dec, F32), jnp.asarray(qd, F32), jnp.asarray(kd, F32)


def _retention(q, k, v, g, tables, b, s):
    c = RET_CHUNK
    rows = c * RET_STEP_CHUNKS
    cc, sc, cr, sr, dec, qd, kd = tables

    def row(width):
        return pl.BlockSpec((b, rows, width), lambda ci: (0, ci, 0))

    def seq(a):
        return a.reshape(b, s, a.shape[-1])

    out = pl.pallas_call(
        _ret_body,
        grid=(s // rows,),
        in_specs=[row(RET_QK_W), row(RET_QK_W), row(RET_V_W), row(RET_V_W),
                  _resident((s // c, RET_QK_W)), _resident((s // c, RET_QK_W)),
                  _resident((c, RET_QK_W)), _resident((c, RET_QK_W)),
                  _resident((RET_HEADS, c, c)), _resident((c, RET_QK_W)), _resident((c, RET_QK_W))],
        out_specs=row(RET_V_W),
        out_shape=jax.ShapeDtypeStruct((b, s, RET_V_W), BF16),
        scratch_shapes=[pltpu.VMEM((b, RET_QK_W, RET_V_DIM), F32)],
        compiler_params=pltpu.CompilerParams(dimension_semantics=("arbitrary",), vmem_limit_bytes=VMEM_LIMIT),
        name="retention",
    )(seq(q), seq(k), seq(v), seq(g), cc, sc, cr, sr, dec, qd, kd)
    return out.reshape(b * s, RET_V_W)


def _merge_body(x_ref, ysb_ref, yret_ref, u_ref, halo_ref, gate_ref, wsb_ref, wret_ref, wpool_ref, pscale_ref,
                wbp_ref, wout_ref, o_ref, *, tiles_per_seq):
    tm = ROW_TILE
    i = pl.program_id(0)
    first = (i % tiles_per_seq) == 0
    u = u_ref[...]
    halo = jnp.where(first, 0.0, halo_ref[...])
    ext = jnp.concatenate([halo, u], axis=0)
    s2 = ext + pltpu.roll(ext, 1, 0)
    s4 = s2 + pltpu.roll(s2, 2, 0)
    s8 = s4 + pltpu.roll(s4, 4, 0)
    s16 = s8 + pltpu.roll(s8, 8, 0)
    lane = lax.broadcasted_iota(jnp.int32, (tm, POOL_W), 1)
    grp = lane // POOL_GROUP_DIM
    h0 = POOL_HALO
    win_sum = jnp.where(grp == 0, s2[h0:], jnp.where(grp == 1, s4[h0:], jnp.where(grp == 2, s8[h0:], s16[h0:])))
    window = jnp.where(grp == 0, POOL_WINDOWS[0],
                       jnp.where(grp == 1, POOL_WINDOWS[1], jnp.where(grp == 2, POOL_WINDOWS[2], POOL_WINDOWS[3])))
    t_pos = (i % tiles_per_seq) * tm + lax.broadcasted_iota(jnp.int32, (tm, POOL_W), 0)
    count = jnp.minimum(t_pos + 1, window).astype(F32)
    pooled = (win_sum / count - u).astype(BF16)
    y_pool = jnp.dot(pooled, wpool_ref[...], preferred_element_type=F32) * pscale_ref[...]

    def gated2(c, y):
        return y + jnp.tanh(gate_ref[:, c * D_MODEL:(c + 1) * D_MODEL].astype(F32)) * y

    y_sb = ysb_ref[0].T.astype(BF16)
    merged2 = gated2(0, jnp.dot(y_sb, wsb_ref[...], preferred_element_type=F32))
    merged2 += gated2(1, jnp.dot(yret_ref[...], wret_ref[...], preferred_element_type=F32))
    merged2 += gated2(2, jnp.dot(y_pool.astype(BF16), wbp_ref[...], preferred_element_type=F32))
    merged = (0.5 * merged2).astype(BF16)
    o_ref[...] = x_ref[...] + jnp.dot(merged, wout_ref[...], preferred_element_type=F32)


def _merge(x2, y_sb, y_ret, u, gate, wsb, wret, wpool_bd, pscale, wbp, wout, s, l):
    t = x2.shape[0]
    tm = ROW_TILE
    halo_per_tile = tm // POOL_HALO
    tps = s // tm

    def row(width):
        return pl.BlockSpec((tm, width), lambda i: (i, 0))

    halo = pl.BlockSpec((POOL_HALO, POOL_W), lambda i: (jnp.maximum(i * halo_per_tile - 1, 0), 0))
    return pl.pallas_call(
        functools.partial(_merge_body, tiles_per_seq=s // tm),
        grid=(t // tm,),
        in_specs=[row(D_MODEL), pl.BlockSpec((1, SB_W, tm), lambda i: (i // tps, 0, i % tps)),
                  row(RET_V_W), row(POOL_W), halo, row(N_BRANCHES * D_MODEL),
                  _layer(l, (SB_W, D_MODEL)), _layer(l, (RET_V_W, D_MODEL)), _resident((POOL_W, POOL_W)),
                  _resident((1, POOL_W)), _layer(l, (POOL_W, D_MODEL)), _layer(l, (D_MODEL, D_MODEL))],
        out_specs=row(D_MODEL),
        out_shape=jax.ShapeDtypeStruct((t, D_MODEL), F32),
        compiler_params=pltpu.CompilerParams(dimension_semantics=("parallel",), vmem_limit_bytes=VMEM_LIMIT),
        name="merge_out",
    )(x2, y_sb, y_ret, u, u, gate, wsb, wret, wpool_bd, pscale, wbp, wout)


def _block_diag(w_pool):
    out = jnp.zeros((POOL_W, POOL_W), w_pool.dtype)
    for g in range(POOL_GROUPS):
        sl = slice(g * POOL_GROUP_DIM, (g + 1) * POOL_GROUP_DIM)
        out = out.at[sl, sl].set(w_pool[g])
    return out


def kernel(x, g_ffn1, w1_ffn1, w3_ffn1, w2_ffn1, g_mix, w_in, w_branch_sb, w_branch_ret, w_branch_pool, w_pool,
           pool_scale, w_out, g_ffn2, w1_ffn2, w3_ffn2, w2_ffn2, g_final):
    b, s, d = x.shape
    assert d == D_MODEL and s % SB_Q_TILE == 0 and s % (RET_CHUNK * RET_STEP_CHUNKS) == 0 and s % ROW_TILE == 0
    depth = g_ffn1.shape[0]
    t = b * s
    assert t % FFN_ROW_TILE == 0
    x2 = x.reshape(t, d)
    tables = _ret_tables(s)
    gf = g_final.reshape(1, d)

    def bf(w):
        return w.astype(BF16)

    w1a, w3a, w2a = bf(w1_ffn1), bf(w3_ffn1), bf(w2_ffn1)
    w1b, w3b, w2b = bf(w1_ffn2), bf(w3_ffn2), bf(w2_ffn2)
    w_in_b, w_out_b = bf(w_in), bf(w_out)
    w_sb, w_ret, w_bp = bf(w_branch_sb), bf(w_branch_ret), bf(w_branch_pool)

    for l in range(depth):
        x2 = _ffn(x2, g_ffn1[l].reshape(1, d), w1a, w3a, w2a, gf, False, l)

        q_t, k_p, v_t, q_r, k_r, v_r, g_r, u_p, gate = _proj(x2, g_mix[l].reshape(1, d), w_in_b, b, s, l)
        y_sb = _sb_attention(q_t, k_p, v_t)

        y_ret = _retention(q_r, k_r, v_r, g_r, tables, b, s)

        x2 = _merge(x2, y_sb, y_ret, u_p, gate, w_sb, w_ret, bf(_block_diag(w_pool[l])),
                    pool_scale[l].reshape(1, POOL_W), w_bp, w_out_b, s, l)

        x2 = _ffn(x2, g_ffn2[l].reshape(1, d), w1b, w3b, w2b, gf, l == depth - 1, l)
    return x2.reshape(b, s, d)
```

```python
import functools
import math

import numpy as np
import jax
import jax.numpy as jnp
from jax import lax
from jax.experimental import pallas as pl
from jax.experimental.pallas import tpu as pltpu

F32 = jnp.float32
BF16 = jnp.bfloat16

D_MODEL = 1024
SB_HEADS = 4
SB_HEAD_DIM = 64
RET_HEADS = 4
RET_QK_DIM = 64
RET_V_DIM = 128
ROPE_BASE = 10000.0
POOL_GROUPS = 4
POOL_GROUP_DIM = 64
POOL_WINDOWS = (2, 4, 8, 16)
N_BRANCHES = 3
D_FF = 2816
EPS = 1e-6

SB_W = SB_HEADS * SB_HEAD_DIM
RET_QK_W = RET_HEADS * RET_QK_DIM
RET_V_W = RET_HEADS * RET_V_DIM
POOL_W = POOL_GROUPS * POOL_GROUP_DIM
D_IN = SB_W * 3 + RET_QK_W * 2 + RET_V_W * 2 + POOL_W + N_BRANCHES * D_MODEL

OFF_QSB = 0
OFF_KSB = OFF_QSB + SB_W
OFF_VSB = OFF_KSB + SB_W
OFF_QR = OFF_VSB + SB_W
OFF_KR = OFF_QR + RET_QK_W
OFF_VR = OFF_KR + RET_QK_W
OFF_GR = OFF_VR + RET_V_W
OFF_UP = OFF_GR + RET_V_W
OFF_GATE = OFF_UP + POOL_W

SUBLANES = 8
LANES = 128
MXU_DIM = 256
VMEM_LIMIT = 56 * 1024 * 1024

ROW_TILE = 1024
FFN_ROW_TILE = 1024
FF_CHUNK = 256
SB_Q_TILE = 2048
SB_K_TILE = 128
SB_NEAR_BLOCKS = 3
SB_HEADS_PER_STEP = 2
SB_STRANDS = SB_K_TILE // SUBLANES
RET_CHUNK = 256
RET_STEP_CHUNKS = 4
POOL_HALO = 16


def _resident(shape):
    nd = len(shape)
    return pl.BlockSpec(shape, lambda *_: (0,) * nd, pipeline_mode=pl.Buffered(1))


def _layer(l, shape):
    nd = len(shape)
    return pl.BlockSpec((None,) + tuple(shape), lambda *_: (l,) + (0,) * nd, pipeline_mode=pl.Buffered(1))


def _rms(x, g):
    ms = jnp.mean(x * x, axis=-1, keepdims=True)
    return x * lax.rsqrt(ms + EPS) * g


def _ffn_body(x_ref, g_ref, w1_ref, w3_ref, w2_ref, gf_ref, o_ref, acc_ref, *, final_norm):
    x = x_ref[...]
    h = _rms(x, g_ref[...]).astype(BF16)
    for c in range(D_FF // FF_CHUNK):
        sl = slice(c * FF_CHUNK, (c + 1) * FF_CHUNK)
        a = jnp.dot(h, w1_ref[:, sl], preferred_element_type=F32)
        b = jnp.dot(h, w3_ref[:, sl], preferred_element_type=F32)
        act = (a * jax.nn.sigmoid(a) * b).astype(BF16)
        part = jnp.dot(act, w2_ref[sl, :], preferred_element_type=F32)
        if c == 0:
            acc_ref[...] = part
        else:
            acc_ref[...] += part
    y = x + 0.5 * acc_ref[...]
    if final_norm:
        y = _rms(y, gf_ref[...])
    o_ref[...] = y


def _ffn(x2, g, w1, w3, w2, g_final, final_norm, l):
    t = x2.shape[0]
    tm = FFN_ROW_TILE
    row = pl.BlockSpec((tm, D_MODEL), lambda i: (i, 0))
    return pl.pallas_call(
        functools.partial(_ffn_body, final_norm=final_norm),
        grid=(t // tm,),
        in_specs=[row, _resident((1, D_MODEL)), _layer(l, (D_MODEL, D_FF)), _layer(l, (D_MODEL, D_FF)),
                  _layer(l, (D_FF, D_MODEL)), _resident((1, D_MODEL))],
        out_specs=row,
        out_shape=jax.ShapeDtypeStruct((t, D_MODEL), F32),
        scratch_shapes=[pltpu.VMEM((tm, D_MODEL), F32)],
        compiler_params=pltpu.CompilerParams(dimension_semantics=("parallel",), vmem_limit_bytes=VMEM_LIMIT),
        name="ffn_half",
    )(x2, g, w1, w3, w2, g_final)


def _proj_body(x_ref, g_ref, w_ref, qsb_ref, ksb_ref, vsb_ref, qr_ref, kr_ref, vr_ref, gr_ref, up_ref, gate_ref,
               perm_scr):
    h = _rms(x_ref[...], g_ref[...]).astype(BF16)

    def mm(lo, width):
        return jnp.dot(h, w_ref[:, lo:lo + width], preferred_element_type=F32)

    def stage(vals):
        for c in range(SB_W // LANES):
            perm_scr[c] = vals[:, c * LANES:(c + 1) * LANES]

    def strand_major(kt):
        base = kt * SB_K_TILE
        return jnp.concatenate(
            [jnp.concatenate([perm_scr[c, pl.ds(base + i, SUBLANES, stride=SB_STRANDS), :]
                              for i in range(SB_STRANDS)], axis=0) for c in range(SB_W // LANES)], axis=1)

    qsb_ref[0] = (mm(OFF_QSB, SB_W) * (SB_HEAD_DIM ** -0.5 * 0.5)).T.astype(BF16)
    stage(mm(OFF_KSB, SB_W))
    for kt in range(ROW_TILE // SB_K_TILE):
        k_tile = strand_major(kt).astype(BF16)
        for hd in range(SB_HEADS):
            ksb_ref[0, hd, kt] = k_tile[:, hd * SB_HEAD_DIM:(hd + 1) * SB_HEAD_DIM]
    stage(mm(OFF_VSB, SB_W))
    for kt in range(ROW_TILE // SB_K_TILE):
        vsb_ref[0, kt] = strand_major(kt).T.astype(BF16)
    qr_ref[...] = mm(OFF_QR, RET_QK_W)
    kr_ref[...] = mm(OFF_KR, RET_QK_W)
    vr_ref[...] = mm(OFF_VR, RET_V_W).astype(BF16)
    gr_ref[...] = mm(OFF_GR, RET_V_W)
    up_ref[...] = mm(OFF_UP, POOL_W)
    for c in range(N_BRANCHES):
        gate_ref[:, c * D_MODEL:(c + 1) * D_MODEL] = (0.5 * mm(OFF_GATE + c * D_MODEL, D_MODEL)).astype(BF16)


def _proj(x2, g, w_in, b, s, l):
    t = x2.shape[0]
    tm = ROW_TILE
    tps = s // tm
    kt_per_tile = tm // SB_K_TILE
    nkt = s // SB_K_TILE

    def row(width):
        return pl.BlockSpec((tm, width), lambda i: (i, 0))

    key_tiles = pl.BlockSpec((1, SB_HEADS, kt_per_tile, SB_K_TILE, SB_HEAD_DIM),
                             lambda i: (i // tps, 0, i % tps, 0, 0))
    val_tiles = pl.BlockSpec((1, kt_per_tile, SB_W, SB_K_TILE), lambda i: (i // tps, i % tps, 0, 0))
    widths = (RET_QK_W, RET_QK_W, RET_V_W, RET_V_W, POOL_W, N_BRANCHES * D_MODEL)
    dtypes = (F32, F32, BF16, F32, F32, BF16)
    return pl.pallas_call(
        _proj_body,
        grid=(t // tm,),
        in_specs=[row(D_MODEL), _resident((1, D_MODEL)), _layer(l, (D_MODEL, D_IN))],
        out_specs=[pl.BlockSpec((1, SB_W, tm), lambda i: (i // tps, 0, i % tps)), key_tiles, val_tiles]
                  + [row(w) for w in widths],
        out_shape=[jax.ShapeDtypeStruct((b, SB_W, s), BF16),
                   jax.ShapeDtypeStruct((b, SB_HEADS, nkt, SB_K_TILE, SB_HEAD_DIM), BF16),
                   jax.ShapeDtypeStruct((b, nkt, SB_W, SB_K_TILE), BF16)]
                  + [jax.ShapeDtypeStruct((t, w), d) for w, d in zip(widths, dtypes)],
        scratch_shapes=[pltpu.VMEM((SB_W // LANES, tm, LANES), F32)],
        compiler_params=pltpu.CompilerParams(dimension_semantics=("parallel",), vmem_limit_bytes=VMEM_LIMIT),
        name="mix_proj",
    )(x2, g, w_in)


def _sb_body(q_ref, k_ref, v_ref, d_ref, o_ref, z_scr, p_scr, w_scr, o_scr):
    qi = pl.program_id(2)
    tq = SB_Q_TILE
    n_diag = tq // SB_K_TILE
    n_tiles = (qi + 1) * n_diag
    o_scr[...] = jnp.zeros_like(o_scr)

    heads = range(SB_HEADS_PER_STEP)

    def head_rows(g):
        return slice(g * SB_HEAD_DIM, (g + 1) * SB_HEAD_DIM)

    def logits(g, j, slot, lo=0, hi=tq):
        z_scr[g, slot, :, lo:hi] = jnp.dot(k_ref[0, g, j], q_ref[0, head_rows(g), lo:hi],
                                           preferred_element_type=F32)

    def weights(g, j, slot, carry, masked, lo=0, hi=tq):
        width = hi - lo
        sub = lax.broadcasted_iota(jnp.int32, (SUBLANES, width), 0)
        thr = qi * tq - j * SB_K_TILE
        acc = jnp.ones((SUBLANES, width), F32)
        for i in range(SB_STRANDS - 1, -1, -1):
            rows = slice(i * SUBLANES, (i + 1) * SUBLANES)
            th = jnp.tanh(z_scr[g, slot, rows, lo:hi])
            if masked:
                th = jnp.where(d_ref[rows, lo:hi] < thr, th, -1.0)
            half = 0.5 * th
            p_scr[g, rows, lo:hi] = (0.5 + half) * acc
            acc = acc * (0.5 - half)
        y = acc
        for sh in (1, 2, 4):
            y = y * jnp.where(sub + sh < SUBLANES, pltpu.roll(y, SUBLANES - sh, 0), 1.0)
        off = jnp.where(sub + 1 < SUBLANES, pltpu.roll(y, SUBLANES - 1, 0), 1.0) * carry[:, lo:hi]
        off2 = jnp.concatenate([off, off], axis=0)
        for i in range(SB_K_TILE // (2 * SUBLANES)):
            rows = slice(i * 2 * SUBLANES, (i + 1) * 2 * SUBLANES)
            w_scr[g, slot, rows, lo:hi] = (p_scr[g, rows, lo:hi] * off2).astype(BF16)
        new = carry[:, lo:hi] * jnp.broadcast_to(y[0:1, :], (SUBLANES, width))
        parts = ([carry[:, :lo]] if lo > 0 else []) + [new] + ([carry[:, hi:]] if hi < tq else [])
        return new if len(parts) == 1 else jnp.concatenate(parts, axis=1)

    def values(g, j, slot, lo=0, hi=tq):
        o_scr[head_rows(g), lo:hi] += jnp.dot(v_ref[0, j, head_rows(g), :], w_scr[g, slot, :, lo:hi],
                                              preferred_element_type=F32)

    def first_lane(m):
        return (n_diag - 1 - m) * SB_K_TILE

    def near_lanes(m):
        return first_lane(m), min(first_lane(m) + SB_NEAR_BLOCKS * SB_K_TILE, tq)

    def diag_weights(g, m, slot, carry):
        lo, hi = near_lanes(m)
        carry = weights(g, top - m, slot, carry, True, lo, lo + SB_K_TILE)
        if lo + SB_K_TILE < hi:
            carry = weights(g, top - m, slot, carry, False, lo + SB_K_TILE, hi)
        return carry

    def live(carry, lo, hi):
        return functools.reduce(jnp.maximum, [jnp.max(c[0:1, lo:hi]) for c in carry]) > 0.0

    top = n_tiles - 1
    carry = [jnp.ones((SUBLANES, tq), F32) for _ in heads]
    for g in heads:
        logits(g, top, 0, *near_lanes(0))
    for p in range(n_diag // 2):
        m = 2 * p
        for g in heads:
            logits(g, top - m - 1, 1, *near_lanes(m + 1))
            carry[g] = diag_weights(g, m, 0, carry[g])
            if p > 0:
                values(g, top - m + 1, 1, *near_lanes(m - 1))
        for g in heads:
            logits(g, jnp.maximum(top - m - 2, 0), 0, *(near_lanes(m + 2) if m + 2 < n_diag else (0, tq)))
            carry[g] = diag_weights(g, m + 1, 1, carry[g])
            values(g, top - m, 0, *near_lanes(m))
    for g in heads:
        values(g, top - n_diag + 1, 1, *near_lanes(n_diag - 1))

    if n_diag > SB_NEAR_BLOCKS:
        def far_tiles(carry):
            carry = list(carry)
            for m in range(SB_NEAR_BLOCKS, n_diag):
                lo = first_lane(m) + SB_NEAR_BLOCKS * SB_K_TILE
                for g in heads:
                    logits(g, top - m, 1, lo, tq)
                    carry[g] = weights(g, top - m, 1, carry[g], False, lo, tq)
                    values(g, top - m, 1, lo, tq)
            return tuple(carry)

        carry = list(lax.cond(live(carry, SB_NEAR_BLOCKS * SB_K_TILE, tq), far_tiles, lambda c: c, tuple(carry)))

    narrow_width = (SB_NEAR_BLOCKS - 1) * SB_K_TILE

    def pair(state, lo, hi):
        p, carry, _ = state
        carry = list(carry)
        j = top - 2 * p
        for g in heads:
            logits(g, j - 1, 1, lo, hi)
            carry[g] = weights(g, j, 0, carry[g], False, lo, hi)
        alive = live(carry, lo, hi)
        for g in heads:
            values(g, j, 0, lo, hi)
            logits(g, jnp.maximum(j - 2, 0), 0, lo, hi)
            carry[g] = weights(g, j - 1, 1, carry[g], False, lo, hi)
            values(g, j - 1, 1, lo, hi)
        return p + 1, tuple(carry), alive

    n_pairs = n_tiles // 2
    narrow = jnp.logical_not(live(carry, narrow_width, tq))
    state = (jnp.int32(n_diag // 2), tuple(carry), jnp.bool_(True))
    state = lax.while_loop(lambda st: (st[0] < n_pairs) & st[2] & narrow,
                           functools.partial(pair, lo=0, hi=narrow_width), state)
    lax.while_loop(lambda st: (st[0] < n_pairs) & st[2], functools.partial(pair, lo=0, hi=tq), state)
    o_ref[0] = o_scr[...]


def _sb_attention(q_t, k_p, v_t):
    b, _, s = q_t.shape
    dh = SB_HEAD_DIM
    nkt = s // SB_K_TILE
    tq = SB_Q_TILE
    r = np.arange(SB_K_TILE)
    key_in_tile = (r % SUBLANES) * SB_STRANDS + r // SUBLANES
    dmat = jnp.asarray(key_in_tile[:, None] - np.arange(tq)[None, :], dtype=jnp.int32)
    hps = SB_HEADS_PER_STEP
    return pl.pallas_call(
        _sb_body,
        grid=(b, SB_HEADS // hps, s // tq),
        in_specs=[
            pl.BlockSpec((1, hps * dh, tq), lambda bi, hi, qi: (bi, hi, qi)),
            pl.BlockSpec((1, hps, nkt, SB_K_TILE, dh), lambda bi, hi, qi: (bi, hi, 0, 0, 0)),
            pl.BlockSpec((1, nkt, hps * dh, SB_K_TILE), lambda bi, hi, qi: (bi, 0, hi, 0)),
            _resident((SB_K_TILE, tq)),
        ],
        out_specs=pl.BlockSpec((1, hps * dh, tq), lambda bi, hi, qi: (bi, hi, qi)),
        out_shape=jax.ShapeDtypeStruct((b, SB_W, s), F32),
        scratch_shapes=[pltpu.VMEM((hps, 2, SB_K_TILE, tq), F32), pltpu.VMEM((hps, SB_K_TILE, tq), F32),
                        pltpu.VMEM((hps, 2, SB_K_TILE, tq), BF16), pltpu.VMEM((hps * dh, tq), F32)],
        compiler_params=pltpu.CompilerParams(dimension_semantics=("parallel", "parallel", "arbitrary"),
                                             vmem_limit_bytes=VMEM_LIMIT),
        name="stick_breaking",
    )(q_t, k_p, v_t, dmat)


def _ret_log_gamma(h):
    return math.log(1.0 - 2.0 ** (-5.0 - h))


def _ret_body(q_ref, k_ref, v_ref, g_ref, cc_ref, sc_ref, cr_ref, sr_ref, dec_ref, qd_ref, kd_ref, o_ref, st_scr):
    c = RET_CHUNK
    step = pl.program_id(0)

    @pl.when(step == 0)
    def _():
        st_scr[...] = jnp.zeros_like(st_scr)

    lane = lax.broadcasted_iota(jnp.int32, (c, RET_QK_W), 1)
    upper = (lane % RET_QK_DIM) >= RET_QK_DIM // 2

    cos_sin = []
    for ci in range(RET_STEP_CHUNKS):
        chunk = step * RET_STEP_CHUNKS + ci
        ca, sa = cc_ref[pl.ds(chunk, 1), :], sc_ref[pl.ds(chunk, 1), :]
        cos_sin.append((ca * cr_ref[...] - sa * sr_ref[...], sa * cr_ref[...] + ca * sr_ref[...]))

    for bi in range(q_ref.shape[0]):
        st = st_scr[bi]
        for ci in range(RET_STEP_CHUNKS):
            rs = slice(ci * c, (ci + 1) * c)
            cos, sin = cos_sin[ci]

            def rot(x):
                partner = jnp.where(upper, pltpu.roll(x, RET_QK_DIM // 2, 1),
                                    -pltpu.roll(x, RET_QK_W - RET_QK_DIM // 2, 1))
                return x * cos + partner * sin

            q = rot(q_ref[bi, rs, :])
            k = rot(k_ref[bi, rs, :])
            kb = k.astype(BF16)
            qdb = (q * qd_ref[...]).astype(BF16)
            kdt = (k * kd_ref[...]).T.astype(BF16)
            v = v_ref[bi, rs, :]
            stb = st.astype(BF16)
            kv_all = jnp.dot(kdt, v, preferred_element_type=F32)
            new_st = []
            for h in range(RET_HEADS):
                head = (lane >= h * RET_QK_DIM) & (lane < (h + 1) * RET_QK_DIM)
                qh = jnp.where(head, q, 0.0).astype(BF16)
                scores = lax.dot_general(qh, kb, (((1,), (1,)), ((), ())), preferred_element_type=F32)
                scores = (scores * dec_ref[h]).astype(BF16)
                vh = v[:, h * RET_V_DIM:(h + 1) * RET_V_DIM]
                intra = jnp.dot(scores, vh, preferred_element_type=F32)
                qdh = jnp.where(head, qdb, jnp.zeros_like(qdb))
                cross = jnp.dot(qdh, stb, preferred_element_type=F32)
                y = intra + cross
                y = y * lax.rsqrt(jnp.mean(y * y, axis=-1, keepdims=True) + EPS)
                gh = g_ref[bi, rs, h * RET_V_DIM:(h + 1) * RET_V_DIM]
                o_ref[bi, rs, h * RET_V_DIM:(h + 1) * RET_V_DIM] = (y * (gh * jax.nn.sigmoid(gh))).astype(BF16)
                rows = slice(h * RET_QK_DIM, (h + 1) * RET_QK_DIM)
                new_st.append(math.exp(c * _ret_log_gamma(h)) * st[rows, :]
                              + kv_all[rows, h * RET_V_DIM:(h + 1) * RET_V_DIM])
            st = jnp.concatenate(new_st, axis=0)
        st_scr[bi] = st


def _ret_tables(s):
    c = RET_CHUNK
    half = RET_QK_DIM // 2
    inv_freq = ROPE_BASE ** (-jnp.arange(half, dtype=F32) / half)

    def cos_sin(pos):
        ang = pos[:, None] * inv_freq[None, :]
        return (jnp.tile(jnp.cos(ang), (1, 2 * RET_HEADS)), jnp.tile(jnp.sin(ang), (1, 2 * RET_HEADS)))

    cc, sc = cos_sin(jnp.arange(s // c, dtype=F32) * c)
    cr, sr = cos_sin(jnp.arange(c, dtype=F32))

    idx = np.arange(c, dtype=np.float64)
    lg = np.array([_ret_log_gamma(h) for h in range(RET_HEADS)])
    diff = idx[:, None] - idx[None, :]
    scale = RET_QK_DIM ** -0.5
    dec = np.where(diff >= 0, np.exp(np.maximum(diff, 0.0)[None] * lg[:, None, None]), 0.0) * scale
    qd = np.repeat(np.exp((idx + 1)[:, None] * lg[None, :]), RET_QK_DIM, axis=1)
    kd = np.repeat(np.exp((c - 1 - idx)[:, None] * lg[None, :]), RET_QK_DIM, axis=1) * scale
    return cc, sc, cr, sr, jnp.asarray(dec, F32), jnp.asarray(qd, F32), jnp.asarray(kd, F32)


def _retention(q, k, v, g, tables, b, s):
    c = RET_CHUNK
    rows = c * RET_STEP_CHUNKS
    cc, sc, cr, sr, dec, qd, kd = tables

    def row(width):
        return pl.BlockSpec((b, rows, width), lambda ci: (0, ci, 0))

    def seq(a):
        return a.reshape(b, s, a.shape[-1])

    out = pl.pallas_call(
        _ret_body,
        grid=(s // rows,),
        in_specs=[row(RET_QK_W), row(RET_QK_W), row(RET_V_W), row(RET_V_W),
                  _resident((s // c, RET_QK_W)), _resident((s // c, RET_QK_W)),
                  _resident((c, RET_QK_W)), _resident((c, RET_QK_W)),
                  _resident((RET_HEADS, c, c)), _resident((c, RET_QK_W)), _resident((c, RET_QK_W))],
        out_specs=row(RET_V_W),
        out_shape=jax.ShapeDtypeStruct((b, s, RET_V_W), BF16),
        scratch_shapes=[pltpu.VMEM((b, RET_QK_W, RET_V_DIM), F32)],
        compiler_params=pltpu.CompilerParams(dimension_semantics=("arbitrary",), vmem_limit_bytes=VMEM_LIMIT),
        name="retention",
    )(seq(q), seq(k), seq(v), seq(g), cc, sc, cr, sr, dec, qd, kd)
    return out.reshape(b * s, RET_V_W)


def _merge_body(x_ref, ysb_ref, yret_ref, u_ref, halo_ref, gate_ref, wsb_ref, wret_ref, wpool_ref, pscale_ref,
                wbp_ref, wout_ref, o_ref, *, tiles_per_seq):
    tm = ROW_TILE
    i = pl.program_id(0)
    first = (i % tiles_per_seq) == 0
    u = u_ref[...]
    halo = jnp.where(first, 0.0, halo_ref[...])
    ext = jnp.concatenate([halo, u], axis=0)
    s2 = ext + pltpu.roll(ext, 1, 0)
    s4 = s2 + pltpu.roll(s2, 2, 0)
    s8 = s4 + pltpu.roll(s4, 4, 0)
    s16 = s8 + pltpu.roll(s8, 8, 0)
    lane = lax.broadcasted_iota(jnp.int32, (tm, POOL_W), 1)
    grp = lane // POOL_GROUP_DIM
    h0 = POOL_HALO
    win_sum = jnp.where(grp == 0, s2[h0:], jnp.where(grp == 1, s4[h0:], jnp.where(grp == 2, s8[h0:], s16[h0:])))
    window = jnp.where(grp == 0, POOL_WINDOWS[0],
                       jnp.where(grp == 1, POOL_WINDOWS[1], jnp.where(grp == 2, POOL_WINDOWS[2], POOL_WINDOWS[3])))
    t_pos = (i % tiles_per_seq) * tm + lax.broadcasted_iota(jnp.int32, (tm, POOL_W), 0)
    count = jnp.minimum(t_pos + 1, window).astype(F32)
    pooled = (win_sum / count - u).astype(BF16)
    y_pool = jnp.dot(pooled, wpool_ref[...], preferred_element_type=F32) * pscale_ref[...]

    def gated2(c, y):
        return y + jnp.tanh(gate_ref[:, c * D_MODEL:(c + 1) * D_MODEL].astype(F32)) * y

    y_sb = ysb_ref[0].T.astype(BF16)
    merged2 = gated2(0, jnp.dot(y_sb, wsb_ref[...], preferred_element_type=F32))
    merged2 += gated2(1, jnp.dot(yret_ref[...], wret_ref[...], preferred_element_type=F32))
    merged2 += gated2(2, jnp.dot(y_pool.astype(BF16), wbp_ref[...], preferred_element_type=F32))
    merged = (0.5 * merged2).astype(BF16)
    o_ref[...] = x_ref[...] + jnp.dot(merged, wout_ref[...], preferred_element_type=F32)


def _merge(x2, y_sb, y_ret, u, gate, wsb, wret, wpool_bd, pscale, wbp, wout, s, l):
    t = x2.shape[0]
    tm = ROW_TILE
    halo_per_tile = tm // POOL_HALO
    tps = s // tm

    def row(width):
        return pl.BlockSpec((tm, width), lambda i: (i, 0))

    halo = pl.BlockSpec((POOL_HALO, POOL_W), lambda i: (jnp.maximum(i * halo_per_tile - 1, 0), 0))
    return pl.pallas_call(
        functools.partial(_merge_body, tiles_per_seq=s // tm),
        grid=(t // tm,),
        in_specs=[row(D_MODEL), pl.BlockSpec((1, SB_W, tm), lambda i: (i // tps, 0, i % tps)),
                  row(RET_V_W), row(POOL_W), halo, row(N_BRANCHES * D_MODEL),
                  _layer(l, (SB_W, D_MODEL)), _layer(l, (RET_V_W, D_MODEL)), _resident((POOL_W, POOL_W)),
                  _resident((1, POOL_W)), _layer(l, (POOL_W, D_MODEL)), _layer(l, (D_MODEL, D_MODEL))],
        out_specs=row(D_MODEL),
        out_shape=jax.ShapeDtypeStruct((t, D_MODEL), F32),
        compiler_params=pltpu.CompilerParams(dimension_semantics=("parallel",), vmem_limit_bytes=VMEM_LIMIT),
        name="merge_out",
    )(x2, y_sb, y_ret, u, u, gate, wsb, wret, wpool_bd, pscale, wbp, wout)


def _block_diag(w_pool):
    out = jnp.zeros((POOL_W, POOL_W), w_pool.dtype)
    for g in range(POOL_GROUPS):
        sl = slice(g * POOL_GROUP_DIM, (g + 1) * POOL_GROUP_DIM)
        out = out.at[sl, sl].set(w_pool[g])
    return out


def kernel(x, g_ffn1, w1_ffn1, w3_ffn1, w2_ffn1, g_mix, w_in, w_branch_sb, w_branch_ret, w_branch_pool, w_pool,
           pool_scale, w_out, g_ffn2, w1_ffn2, w3_ffn2, w2_ffn2, g_final):
    b, s, d = x.shape
    assert d == D_MODEL and s % SB_Q_TILE == 0 and s % (RET_CHUNK * RET_STEP_CHUNKS) == 0 and s % ROW_TILE == 0
    depth = g_ffn1.shape[0]
    t = b * s
    assert t % FFN_ROW_TILE == 0
    x2 = x.reshape(t, d)
    tables = _ret_tables(s)
    gf = g_final.reshape(1, d)

    def bf(w):
        return w.astype(BF16)

    w1a, w3a, w2a = bf(w1_ffn1), bf(w3_ffn1), bf(w2_ffn1)
    w1b, w3b, w2b = bf(w1_ffn2), bf(w3_ffn2), bf(w2_ffn2)
    w_in_b, w_out_b = bf(w_in), bf(w_out)
    w_sb, w_ret, w_bp = bf(w_branch_sb), bf(w_branch_ret), bf(w_branch_pool)

    for l in range(depth):
        x2 = _ffn(x2, g_ffn1[l].reshape(1, d), w1a, w3a, w2a, gf, False, l)

        q_t, k_p, v_t, q_r, k_r, v_r, g_r, u_p, gate = _proj(x2, g_mix[l].reshape(1, d), w_in_b, b, s, l)
        y_sb = _sb_attention(q_t, k_p, v_t)

        y_ret = _retention(q_r, k_r, v_r, g_r, tables, b, s)

        x2 = _merge(x2, y_sb, y_ret, u_p, gate, w_sb, w_ret, bf(_block_diag(w_pool[l])),
                    pool_scale[l].reshape(1, POOL_W), w_bp, w_out_b, s, l)

        x2 = _ffn(x2, g_ffn2[l].reshape(1, d), w1b, w3b, w2b, gf, l == depth - 1, l)
    return x2.reshape(b, s, d)
```
